```python
import math
import jax, jax.numpy as jnp
from jax import lax
import numpy as np

D_MODEL = 1024
BATCH = 2
SEQ = 8192
DEPTH = 2
DEC_BATCH = 32
DEC_SEQ = 8
PAST_LEN = 8192
PAGE_SIZE = 128

N_AB_LAYERS = (DEPTH + 1) // 2
N_SSD_LAYERS = DEPTH // 2
GMLP_GROUPS = 8
GMLP_GROUP_DIM = D_MODEL // 16
GMLP_WIDTH = GMLP_GROUPS * GMLP_GROUP_DIM
GMLP_CHUNK = 128
MOBA_HEADS = 8
MOBA_HEAD_DIM = D_MODEL // 16
MOBA_WIDTH = MOBA_HEADS * MOBA_HEAD_DIM
MOBA_BLOCK = 256
MOBA_TOPK = 3
MOBA_Q_BLOCK = 128
ROPE_THETA = 10000.0
AB_IN_WIDTH = 2 * GMLP_WIDTH + 3 * MOBA_WIDTH
AB_OUT_WIDTH = GMLP_WIDTH + MOBA_WIDTH
SSD_INNER = 2 * D_MODEL
SSD_HEAD_DIM = 64
SSD_HEADS = SSD_INNER // SSD_HEAD_DIM
SSD_GROUPS = 4
SSD_STATE = 128
SSD_CONV = 4
SSD_CHUNK = 128
SSD_CONV_DIM = SSD_INNER + 2 * SSD_GROUPS * SSD_STATE
SSD_IN_WIDTH = SSD_INNER + SSD_CONV_DIM + SSD_HEADS
PEER_HEADS = 8
PEER_N_KEYS = 128
PEER_N_EXPERTS = PEER_N_KEYS * PEER_N_KEYS
PEER_TOPK = 16
PEER_KEY_DIM = 256
PEER_HALF_DIM = PEER_KEY_DIM // 2
PEER_TOKEN_BLOCK = 256
EPS = 1e-6

kernel_name = 'hybrid_gmlp_moba_ssd_peer_step'


def rms_norm(x, g):
    xf = x.astype(jnp.float32)
    y = xf * lax.rsqrt(jnp.mean(xf * xf, axis=-1, keepdims=True) + EPS)
    return (y * g.astype(jnp.float32)).astype(x.dtype)


def layer_norm(x, g, b):
    xf = x.astype(jnp.float32)
    mu = jnp.mean(xf, axis=-1, keepdims=True)
    var = jnp.mean(jnp.square(xf - mu), axis=-1, keepdims=True)
    y = (xf - mu) * lax.rsqrt(var + EPS) * g.astype(jnp.float32) + b.astype(jnp.float32)
    return y.astype(x.dtype)


def ada_rms(x, c, g, w, b):
    mod = c @ w + b
    shift, scale, gate = jnp.split(mod, 3, axis=-1)
    h = rms_norm(x, g) * (1.0 + scale[:, None]) + shift[:, None]
    return h, gate[:, None]


def rope(x, pos):
    half = x.shape[-1] // 2
    inv = jnp.power(ROPE_THETA, -jnp.arange(half, dtype=jnp.float32) / half)
    ang = pos.astype(jnp.float32)[:, None] * inv[None, :]
    cos = jnp.cos(ang)[None, :, None, :]
    sin = jnp.sin(ang)[None, :, None, :]
    xf = x.astype(jnp.float32)
    x1, x2 = xf[..., :half], xf[..., half:]
    return jnp.concatenate([x1 * cos - x2 * sin, x2 * cos + x1 * sin], axis=-1).astype(x.dtype)


def chunk_mix(v, ws, bs):
    bsz, l, _ = v.shape
    lc = min(l, GMLP_CHUNK)
    vc = v.reshape(bsz, l // lc, lc, GMLP_GROUPS, GMLP_GROUP_DIM)
    causal = jnp.tril(jnp.ones((lc, lc), dtype=bool))[None]
    w = jnp.where(causal, ws[:, :lc, :lc], 0.0).astype(v.dtype)
    out = jnp.einsum('gts,bcsgd->bctgd', w, vc) + bs[:, :lc].T[None, None, :, :, None]
    return out.reshape(bsz, l, GMLP_WIDTH)


def moba_attention(q, k_all, v_all, q_pos):
    bsz, t_len = k_all.shape[0], k_all.shape[1]
    nb = -(-t_len // MOBA_BLOCK)
    pad = nb * MOBA_BLOCK - t_len
    k_blk = jnp.pad(k_all, ((0, 0), (0, pad), (0, 0), (0, 0))).reshape(bsz, nb, MOBA_BLOCK, MOBA_HEADS, MOBA_HEAD_DIM)
    v_blk = jnp.pad(v_all, ((0, 0), (0, pad), (0, 0), (0, 0))).reshape(bsz, nb, MOBA_BLOCK, MOBA_HEADS, MOBA_HEAD_DIM)
    k_mean = jnp.mean(k_blk.astype(jnp.float32), axis=2)
    k_blk = k_blk.transpose(0, 3, 1, 2, 4)
    v_blk = v_blk.transpose(0, 3, 1, 2, 4)
    n_sel = min(MOBA_TOPK, nb)
    scale = MOBA_HEAD_DIM ** -0.5
    bi = jnp.arange(bsz)[:, None, None, None]
    hi = jnp.arange(MOBA_HEADS)[None, :, None, None]
    blk_ids = jnp.arange(nb, dtype=jnp.int32)
    offs = jnp.arange(MOBA_BLOCK, dtype=jnp.int32)

    def attend(q_blk, pos_blk):
        nq = q_blk.shape[1]
        own = pos_blk // MOBA_BLOCK
        gate = jnp.einsum('bqhd,bjhd->bhqj', q_blk.astype(jnp.float32), k_mean)
        gate = jnp.where(blk_ids[None, :] < own[:, None], gate, -jnp.inf)
        top_val, top_idx = lax.top_k(gate, n_sel)
        own_b = jnp.broadcast_to(own[None, None, :, None], (bsz, MOBA_HEADS, nq, 1)).astype(top_idx.dtype)
        sel = jnp.concatenate([top_idx, own_b], axis=-1)
        ok = jnp.concatenate([top_val > -jnp.inf, jnp.ones((bsz, MOBA_HEADS, nq, 1), dtype=bool)], axis=-1)
        kg = k_blk[bi, hi, sel]
        vg = v_blk[bi, hi, sel]
        s = jnp.einsum('bqhd,bhqjkd->bhqjk', q_blk, kg).astype(jnp.float32) * scale
        kpos = sel[..., None] * MOBA_BLOCK + offs
        mask = ok[..., None] & (kpos <= pos_blk[None, None, :, None, None])
        s = jnp.where(mask, s, -jnp.inf)
        p = jax.nn.softmax(s.reshape(bsz, MOBA_HEADS, nq, -1), axis=-1).reshape(s.shape)
        return jnp.einsum('bhqjk,bhqjkd->bqhd', p.astype(vg.dtype), vg)

    nq = q.shape[1]
    qb = math.gcd(nq, MOBA_Q_BLOCK)
    if qb == nq:
        return attend(q, q_pos)
    n_qb = nq // qb
    qs = q.reshape(bsz, n_qb, qb, MOBA_HEADS, MOBA_HEAD_DIM).transpose(1, 0, 2, 3, 4)
    ps = q_pos.reshape(n_qb, qb)
    out = lax.map(lambda a: attend(a[0], a[1]), (qs, ps))
    return out.transpose(1, 0, 2, 3, 4).reshape(bsz, nq, MOBA_HEADS, MOBA_HEAD_DIM)


def ab_mixer(h, pos, k_past, v_past, w_in, w_out, ln_g, ln_b, ws, bs):
    bsz, l, _ = h.shape
    proj = h @ w_in
    u, v, q, k, vb = jnp.split(proj, [GMLP_WIDTH, 2 * GMLP_WIDTH, 2 * GMLP_WIDTH + MOBA_WIDTH,
                                      2 * GMLP_WIDTH + 2 * MOBA_WIDTH], axis=-1)
    u = jax.nn.gelu(u, approximate=False)
    v = layer_norm(jax.nn.gelu(v, approximate=False), ln_g, ln_b)
    a_out = u * chunk_mix(v, ws, bs)
    q = rope(q.reshape(bsz, l, MOBA_HEADS, MOBA_HEAD_DIM), pos)
    k = rope(k.reshape(bsz, l, MOBA_HEADS, MOBA_HEAD_DIM), pos)
    vb = vb.reshape(bsz, l, MOBA_HEADS, MOBA_HEAD_DIM)
    k_all = k if k_past is None else jnp.concatenate([k_past.astype(k.dtype), k], axis=1)
    v_all = vb if v_past is None else jnp.concatenate([v_past.astype(vb.dtype), vb], axis=1)
    b_out = moba_attention(q, k_all, v_all, pos).reshape(bsz, l, MOBA_WIDTH)
    out = jnp.concatenate([a_out, b_out], axis=-1) @ w_out
    lc = min(l, GMLP_CHUNK)
    return out, k, vb, v[:, l - lc:]


def ssd_scan(x, dt, a, bm, cm, h0):
    bsz, l, nh, p = x.shape
    g, n = bm.shape[2], bm.shape[3]
    r = nh // g
    q = math.gcd(l, SSD_CHUNK)
    nc = l // q
    xdt = (x.astype(jnp.float32) * dt[..., None]).reshape(bsz, nc, q, g, r, p)
    adt = (dt * a).reshape(bsz, nc, q, g, r)
    bc = bm.astype(jnp.float32).reshape(bsz, nc, q, g, n)
    cc = cm.astype(jnp.float32).reshape(bsz, nc, q, g, n)
    a_cum = jnp.cumsum(adt, axis=2)
    causal = jnp.tril(jnp.ones((q, q), dtype=bool))[None, None, :, :, None, None]
    seg = a_cum[:, :, :, None] - a_cum[:, :, None, :]
    decay = jnp.exp(jnp.where(causal, seg, -jnp.inf))
    cb = jnp.einsum('bcqgn,bcsgn->bcqsg', cc, bc)
    y_diag = jnp.einsum('bcqsg,bcqsgr,bcsgrp->bcqgrp', cb, decay, xdt)
    to_end = jnp.exp(a_cum[:, :, -1:] - a_cum)
    chunk_states = jnp.einsum('bcsgn,bcsgr,bcsgrp->bcgrpn', bc, to_end, xdt)
    chunk_decay = jnp.exp(a_cum[:, :, -1])

    def step(state, inp):
        cs, cd = inp
        return state * cd[..., None, None] + cs, state

    final, states_in = lax.scan(step, h0.astype(jnp.float32).reshape(bsz, g, r, p, n),
                                (jnp.moveaxis(chunk_states, 1, 0), jnp.moveaxis(chunk_decay, 1, 0)))
    states_in = jnp.moveaxis(states_in, 0, 1)
    y_off = jnp.einsum('bcqgn,bcgrpn,bcqgr->bcqgrp', cc, states_in, jnp.exp(a_cum))
    y = (y_diag + y_off).reshape(bsz, l, nh, p)
    return y, final.reshape(bsz, nh, p, n)


def ssd_mixer(h, ssm0, conv0, w_in, conv_w, conv_b, dt_bias, a_log, d_skip, norm_g, w_out):
    bsz, l, _ = h.shape
    z, xbc, dt = jnp.split(h @ w_in, [SSD_INNER, SSD_INNER + SSD_CONV_DIM], axis=-1)
    if conv0 is None:
        conv0 = jnp.zeros((bsz, SSD_CONV - 1, SSD_CONV_DIM), dtype=xbc.dtype)
    xpad = jnp.concatenate([conv0.astype(xbc.dtype), xbc], axis=1)
    new_conv = xpad[:, xpad.shape[1] - (SSD_CONV - 1):]
    conv = conv_b + xpad[:, 0:l] * conv_w[0]
    for tap in range(1, SSD_CONV):
        conv = conv + xpad[:, tap:tap + l] * conv_w[tap]
    xbc = jax.nn.silu(conv)
    xs, bm, cm = jnp.split(xbc, [SSD_INNER, SSD_INNER + SSD_GROUPS * SSD_STATE], axis=-1)
    xs = xs.reshape(bsz, l, SSD_HEADS, SSD_HEAD_DIM)
    bm = bm.reshape(bsz, l, SSD_GROUPS, SSD_STATE)
    cm = cm.reshape(bsz, l, SSD_GROUPS, SSD_STATE)
    dt = jax.nn.softplus(dt.astype(jnp.float32) + dt_bias.astype(jnp.float32))
    a = -jnp.exp(a_log.astype(jnp.float32))
    if ssm0 is None:
        ssm0 = jnp.zeros((bsz, SSD_HEADS, SSD_HEAD_DIM, SSD_STATE), dtype=jnp.float32)
    y, final = ssd_scan(xs, dt, a, bm, cm, ssm0)
    y = y + d_skip.astype(jnp.float32)[:, None] * xs.astype(jnp.float32)
    y = y.reshape(bsz, l, SSD_INNER).astype(h.dtype)
    y = rms_norm(y * jax.nn.silu(z), norm_g)
    return y @ w_out, final, new_conv


def peer_ffn(h, wq, subkeys, u_tab, v_tab):
    bsz, l, d = h.shape
    n_tok = bsz * l
    blk = math.gcd(n_tok, PEER_TOKEN_BLOCK)

    def one_block(xb):
        q = (xb @ wq).reshape(blk, PEER_HEADS, 2, PEER_HALF_DIM)
        s = jnp.einsum('thcd,hckd->thck', q, subkeys).astype(jnp.float32)
        s1, i1 = lax.top_k(s[:, :, 0], PEER_TOPK)
        s2, i2 = lax.top_k(s[:, :, 1], PEER_TOPK)
        cand = (s1[..., :, None] + s2[..., None, :]).reshape(blk, PEER_HEADS, PEER_TOPK * PEER_TOPK)
        cidx = (i1[..., :, None] * PEER_N_KEYS + i2[..., None, :]).reshape(blk, PEER_HEADS, PEER_TOPK * PEER_TOPK)
        best, pick = lax.top_k(cand, PEER_TOPK)
        eidx = jnp.take_along_axis(cidx, pick, axis=-1)
        g = jax.nn.softmax(best, axis=-1)
        u = u_tab[eidx]
        act = jax.nn.gelu(jnp.einsum('td,thkd->thk', xb, u).astype(jnp.float32), approximate=False)
        return jnp.einsum('thk,thkd->td', (g * act).astype(xb.dtype), v_tab[eidx])

    out = lax.map(one_block, h.reshape(n_tok // blk, blk, d))
    return out.reshape(bsz, l, d)


def run_trunk(x, c, pos0, k_past, v_past, ssm_past, conv_past,
              norm_mix_g, mod_mix_w, mod_mix_b, norm_ffn_g, mod_ffn_w, mod_ffn_b,
              w_in_ab, w_out_ab, gmlp_ln_g, gmlp_ln_b, gmlp_ws, gmlp_bs,
              ssd_w_in, ssd_conv_w, ssd_conv_b, ssd_dt_bias, ssd_a_log, ssd_d, ssd_norm_g, ssd_w_out,
              peer_wq, peer_subkeys, peer_u, peer_v, final_norm_g):
    l = x.shape[1]
    pos = pos0 + jnp.arange(l, dtype=jnp.int32)
    k_rows, v_rows, gv_rows, ssm_new, conv_new = [], [], [], [], []
    for i in range(DEPTH):
        j = i // 2
        h, gate = ada_rms(x, c, norm_mix_g[i], mod_mix_w[i], mod_mix_b[i])
        if i % 2 == 0:
            out, k_new, v_new, gv = ab_mixer(h, pos,
                                             None if k_past is None else k_past[j],
                                             None if v_past is None else v_past[j],
                                             w_in_ab[j], w_out_ab[j], gmlp_ln_g[j], gmlp_ln_b[j],
                                             gmlp_ws[j], gmlp_bs[j])
            k_rows.append(k_new)
            v_rows.append(v_new)
            gv_rows.append(gv)
        else:
            out, s_new, cv_new = ssd_mixer(h,
                                           None if ssm_past is None else ssm_past[j],
                                           None if conv_past is None else conv_past[j],
                                           ssd_w_in[j], ssd_conv_w[j], ssd_conv_b[j], ssd_dt_bias[j],
                                           ssd_a_log[j], ssd_d[j], ssd_norm_g[j], ssd_w_out[j])
            ssm_new.append(s_new)
            conv_new.append(cv_new)
        x = x + gate * out
        h, gate = ada_rms(x, c, norm_ffn_g[i], mod_ffn_w[i], mod_ffn_b[i])
        x = x + gate * peer_ffn(h, peer_wq[i], peer_subkeys[i], peer_u[i], peer_v[i])
    y = rms_norm(x, final_norm_g)
    return y, jnp.stack(k_rows), jnp.stack(v_rows), jnp.stack(gv_rows), jnp.stack(ssm_new), jnp.stack(conv_new)


def setup_inputs(seed: int = 0) -> dict:
    key = jax.random.key(seed)
    ks = jax.random.split(key, 40)
    f32 = jnp.float32

    def nrm(i, shape, scale):
        return jax.random.normal(ks[i], shape, f32) * scale

    n_pages = PAST_LEN // PAGE_SIZE
    n_used = DEC_BATCH * n_pages
    n_pool = n_used + n_used // 4
    page_table = jax.random.permutation(ks[0], n_pool)[:n_used].reshape(DEC_BATCH, n_pages).astype(jnp.int32)
    dt0 = jnp.exp(jax.random.uniform(ks[1], (N_SSD_LAYERS, SSD_HEADS), f32, math.log(1e-3), math.log(1e-1)))
    ssd_dt_bias = dt0 + jnp.log(-jnp.expm1(-dt0))
    ssd_a_log = jnp.log(jax.random.uniform(ks[2], (N_SSD_LAYERS, SSD_HEADS), f32, 1.0, 16.0))
    dm = D_MODEL ** -0.5
    return {
        'x_prompt': nrm(3, (BATCH, SEQ, D_MODEL), 1.0),
        'x_sample': nrm(4, (DEC_BATCH, DEC_SEQ, D_MODEL), 1.0),
        'cache_k': nrm(5, (N_AB_LAYERS, n_pool, PAGE_SIZE, MOBA_HEADS, MOBA_HEAD_DIM), 1.0),
        'cache_v': nrm(6, (N_AB_LAYERS, n_pool, PAGE_SIZE, MOBA_HEADS, MOBA_HEAD_DIM), 1.0),
        'page_table': page_table,
        'state_ssm': nrm(7, (N_SSD_LAYERS, DEC_BATCH, SSD_HEADS, SSD_HEAD_DIM, SSD_STATE), 0.1),
        'state_conv': nrm(8, (N_SSD_LAYERS, DEC_BATCH, SSD_CONV - 1, SSD_CONV_DIM), 1.0),
        'c_prompt': nrm(9, (BATCH, D_MODEL), 1.0),
        'c_sample': nrm(10, (DEC_BATCH, D_MODEL), 1.0),
        'norm_mix_g': 1.0 + nrm(11, (DEPTH, D_MODEL), 0.02),
        'mod_mix_w': nrm(12, (DEPTH, D_MODEL, 3 * D_MODEL), 0.3 * dm),
        'mod_mix_b': nrm(13, (DEPTH, 3 * D_MODEL), 0.02),
        'norm_ffn_g': 1.0 + nrm(14, (DEPTH, D_MODEL), 0.02),
        'mod_ffn_w': nrm(15, (DEPTH, D_MODEL, 3 * D_MODEL), 0.3 * dm),
        'mod_ffn_b': nrm(16, (DEPTH, 3 * D_MODEL), 0.02),
        'w_in_ab': nrm(17, (N_AB_LAYERS, D_MODEL, AB_IN_WIDTH), dm),
        'w_out_ab': nrm(18, (N_AB_LAYERS, AB_OUT_WIDTH, D_MODEL), AB_OUT_WIDTH ** -0.5),
        'gmlp_ln_g': 1.0 + nrm(19, (N_AB_LAYERS, GMLP_WIDTH), 0.02),
        'gmlp_ln_b': nrm(20, (N_AB_LAYERS, GMLP_WIDTH), 0.02),
        'gmlp_ws': nrm(21, (N_AB_LAYERS, GMLP_GROUPS, GMLP_CHUNK, GMLP_CHUNK), GMLP_CHUNK ** -0.5),
        'gmlp_bs': 1.0 + nrm(22, (N_AB_LAYERS, GMLP_GROUPS, GMLP_CHUNK), 0.02),
        'ssd_w_in': nrm(23, (N_SSD_LAYERS, D_MODEL, SSD_IN_WIDTH), dm),
        'ssd_conv_w': nrm(24, (N_SSD_LAYERS, SSD_CONV, SSD_CONV_DIM), 0.5),
        'ssd_conv_b': nrm(25, (N_SSD_LAYERS, SSD_CONV_DIM), 0.02),
        'ssd_dt_bias': ssd_dt_bias,
        'ssd_a_log': ssd_a_log,
        'ssd_d': 1.0 + nrm(26, (N_SSD_LAYERS, SSD_HEADS), 0.1),
        'ssd_norm_g': 1.0 + nrm(27, (N_SSD_LAYERS, SSD_INNER), 0.02),
        'ssd_w_out': nrm(28, (N_SSD_LAYERS, SSD_INNER, D_MODEL), SSD_INNER ** -0.5),
        'peer_wq': nrm(29, (DEPTH, D_MODEL, PEER_HEADS * PEER_KEY_DIM), dm),
        'peer_subkeys': nrm(30, (DEPTH, PEER_HEADS, 2, PEER_N_KEYS, PEER_HALF_DIM), PEER_HALF_DIM ** -0.5),
        'peer_u': nrm(31, (DEPTH, PEER_N_EXPERTS, D_MODEL), dm),
        'peer_v': nrm(32, (DEPTH, PEER_N_EXPERTS, D_MODEL), 1.0),
        'final_norm_g': 1.0 + nrm(33, (D_MODEL,), 0.02),
    }


def reference(x_prompt, x_sample, cache_k, cache_v, page_table, state_ssm, state_conv, c_prompt, c_sample,
              norm_mix_g, mod_mix_w, mod_mix_b, norm_ffn_g, mod_ffn_w, mod_ffn_b,
              w_in_ab, w_out_ab, gmlp_ln_g, gmlp_ln_b, gmlp_ws, gmlp_bs,
              ssd_w_in, ssd_conv_w, ssd_conv_b, ssd_dt_bias, ssd_a_log, ssd_d, ssd_norm_g, ssd_w_out,
              peer_wq, peer_subkeys, peer_u, peer_v, final_norm_g):
    weights = (norm_mix_g, mod_mix_w, mod_mix_b, norm_ffn_g, mod_ffn_w, mod_ffn_b,
               w_in_ab, w_out_ab, gmlp_ln_g, gmlp_ln_b, gmlp_ws, gmlp_bs,
               ssd_w_in, ssd_conv_w, ssd_conv_b, ssd_dt_bias, ssd_a_log, ssd_d, ssd_norm_g, ssd_w_out,
               peer_wq, peer_subkeys, peer_u, peer_v, final_norm_g)
    y_prompt, k_p, v_p, gv_p, ssm_p, conv_p = run_trunk(x_prompt, c_prompt, 0, None, None, None, None, *weights)
    n_dec, n_pages = page_table.shape
    past_len = n_pages * PAGE_SIZE
    k_past = [cache_k[j][page_table].reshape(n_dec, past_len, MOBA_HEADS, MOBA_HEAD_DIM) for j in range(N_AB_LAYERS)]
    v_past = [cache_v[j][page_table].reshape(n_dec, past_len, MOBA_HEADS, MOBA_HEAD_DIM) for j in range(N_AB_LAYERS)]
    y_sample, k_s, v_s, gv_s, ssm_s, conv_s = run_trunk(x_sample, c_sample, past_len, k_past, v_past,
                                                        state_ssm, state_conv, *weights)
    return (y_prompt, y_sample, k_p, v_p, gv_p, ssm_p, conv_p, k_s, v_s, gv_s, ssm_s, conv_s)
```

```python
import functools
import math

import jax
import jax.numpy as jnp
from jax import lax
from jax.experimental import pallas as pl
from jax.experimental.pallas import tpu as pltpu

F32 = jnp.float32
BF16 = jnp.bfloat16
NEG_INF = float("-inf")

EPS = 1e-6
LANES = 128
GMLP_GROUPS = 8
GMLP_CHUNK = 128
MOBA_HEADS = 8
MOBA_HEAD_DIM = 64
MOBA_BLOCK = 256
MOBA_TOPK = 3
PAGE_SIZE = 128
ROPE_THETA = 10000.0
SSD_HEADS = 32
SSD_HEAD_DIM = 64
SSD_GROUPS = 4
SSD_STATE = 128
SSD_CONV = 4
SSD_CHUNK = 128
SSD_INNER = SSD_HEADS * SSD_HEAD_DIM
SSD_BC = SSD_GROUPS * SSD_STATE
SSD_CONV_DIM = SSD_INNER + 2 * SSD_BC
PEER_HEADS = 8
PEER_TOPK = 16
VMEM_LIMIT_BYTES = 56 * 1024 * 1024


def _cparams(*sem):
    return pltpu.CompilerParams(dimension_semantics=sem, vmem_limit_bytes=VMEM_LIMIT_BYTES)


def _dot(a, b):
    return jnp.dot(a, b, preferred_element_type=F32)


def _dot_nt(a, b):
    return lax.dot_general(a, b, (((1,), (1,)), ((), ())), preferred_element_type=F32)


def _iota(shape, dim):
    return lax.broadcasted_iota(jnp.int32, shape, dim)


def _gelu(x):
    return 0.5 * x * (1.0 + lax.erf(x * (2.0 ** -0.5)))


def _split3(x):
    hi = x.astype(BF16)
    r = x - hi.astype(F32)
    mid = r.astype(BF16)
    lo = (r - mid.astype(F32)).astype(BF16)
    return hi, mid, lo


def _dot_exact_rhs01(x, e):
    hi, mid, lo = _split3(x)
    return _dot(hi, e) + _dot(mid, e) + _dot(lo, e)


def _mod_kernel(c_ref, w_ref, b_ref, o_ref):
    o_ref[0] = _dot(c_ref[...].astype(BF16), w_ref[0]) + b_ref[0]


def _mod_all(c, w4, b4):
    bsz, d = c.shape
    rows = -(-bsz // 16) * 16
    cp = jnp.pad(c, ((0, rows - bsz), (0, 0)))
    n = w4.shape[0]
    out = pl.pallas_call(
        _mod_kernel,
        name="adaln_mod",
        out_shape=jax.ShapeDtypeStruct((n, rows, 3 * d), F32),
        grid=(n, 3),
        in_specs=[pl.BlockSpec((rows, d), lambda i, j: (0, 0)),
                  pl.BlockSpec((1, d, d), lambda i, j: (i, 0, j)),
                  pl.BlockSpec((1, 1, d), lambda i, j: (i, 0, j))],
        out_specs=pl.BlockSpec((1, rows, d), lambda i, j: (i, 0, j)),
        compiler_params=_cparams("arbitrary", "arbitrary"),
    )(cp, w4, b4)
    return out[:, :bsz]


def _mod_rows(mod, seq_len, tm):
    bsz, w = mod.shape
    if seq_len % tm == 0:
        return mod[:, None, :], seq_len // tm
    assert tm % seq_len == 0 and (bsz * seq_len) % tm == 0
    return jnp.repeat(mod, seq_len, axis=0).reshape(bsz * seq_len // tm, tm, w), 1


def _adaln_mm_kernel(x_ref, g_ref, sh_ref, sc_ref, w_ref, o_ref, *rest, emit_h):
    if emit_h:
        h_ref, h_scr = rest
    else:
        (h_scr,) = rest

    @pl.when(pl.program_id(1) == 0)
    def _():
        x = x_ref[...]
        y = x * lax.rsqrt(jnp.mean(x * x, axis=-1, keepdims=True) + EPS)
        y = y * g_ref[...]
        h = (y * (1.0 + sc_ref[0]) + sh_ref[0]).astype(BF16)
        h_scr[...] = h
        if emit_h:
            h_ref[...] = h

    o_ref[...] = _dot(h_scr[...], w_ref[...]).astype(o_ref.dtype)


def _adaln_matmul(x, g, mod, seq_len, w, *, tm, tn, out_dtype=F32, emit_h=False):
    t, d = x.shape
    n = w.shape[1]
    assert t % tm == 0 and n % tn == 0
    mod3, tpg = _mod_rows(mod, seq_len, tm)
    r = mod3.shape[1]
    out_shape = [jax.ShapeDtypeStruct((t, n), out_dtype)]
    out_specs = [pl.BlockSpec((tm, tn), lambda i, j: (i, j))]
    if emit_h:
        out_shape.append(jax.ShapeDtypeStruct((t, d), BF16))
        out_specs.append(pl.BlockSpec((tm, d), lambda i, j: (i, 0)))
    res = pl.pallas_call(
        functools.partial(_adaln_mm_kernel, emit_h=emit_h),
        name="adaln_matmul",
        out_shape=out_shape,
        grid=(t // tm, n // tn),
        in_specs=[pl.BlockSpec((tm, d), lambda i, j: (i, 0)),
                  pl.BlockSpec((1, d), lambda i, j: (0, 0)),
                  pl.BlockSpec((1, r, d), lambda i, j: (i // tpg, 0, 0)),
                  pl.BlockSpec((1, r, d), lambda i, j: (i // tpg, 0, 1)),
                  pl.BlockSpec((d, tn), lambda i, j: (0, j))],
        out_specs=out_specs,
        scratch_shapes=[pltpu.VMEM((tm, d), BF16)],
        compiler_params=_cparams("arbitrary", "arbitrary"),
    )(x, g.reshape(1, d), mod3, mod3, w)
    return res if emit_h else res[0]


def _proj_res_kernel(*refs, ks):
    lhs = refs[:len(ks)]
    w_ref, x_ref, gate_ref, o_ref = refs[len(ks):]
    acc = None
    off = 0
    for l_ref, k in zip(lhs, ks):
        part = _dot(l_ref[...].astype(BF16), w_ref[off:off + k, :])
        acc = part if acc is None else acc + part
        off += k
    o_ref[...] = x_ref[...] + gate_ref[0] * acc


def _proj_residual(lhs, w, x, mod, seq_len, *, tm):
    t, d = x.shape
    ks = tuple(l.shape[1] for l in lhs)
    mod3, tpg = _mod_rows(mod, seq_len, tm)
    r = mod3.shape[1]
    in_specs = [pl.BlockSpec((tm, k), lambda i: (i, 0)) for k in ks]
    in_specs += [pl.BlockSpec((sum(ks), d), lambda i: (0, 0)),
                 pl.BlockSpec((tm, d), lambda i: (i, 0)),
                 pl.BlockSpec((1, r, d), lambda i: (i // tpg, 0, 2))]
    return pl.pallas_call(
        functools.partial(_proj_res_kernel, ks=ks),
        name="proj_residual",
        out_shape=jax.ShapeDtypeStruct((t, d), F32),
        grid=(t // tm,),
        in_specs=in_specs,
        out_specs=pl.BlockSpec((tm, d), lambda i: (i, 0)),
        compiler_params=_cparams("arbitrary"),
    )(*lhs, w, x, mod3)


def _res_kernel(x_ref, yt_ref, gate_ref, *rest, final_norm):
    x2 = x_ref[...] + gate_ref[0] * yt_ref[...].T
    if final_norm:
        g_ref, o_ref = rest
        o_ref[...] = x2 * lax.rsqrt(jnp.mean(x2 * x2, axis=-1, keepdims=True) + EPS) * g_ref[...]
    else:
        (o_ref,) = rest
        o_ref[...] = x2


def _residual_t(x, y_t, mod, seq_len, *, tm, final_g=None):
    t, d = x.shape
    mod3, tpg = _mod_rows(mod, seq_len, tm)
    r = mod3.shape[1]
    in_specs = [pl.BlockSpec((tm, d), lambda i: (i, 0)),
                pl.BlockSpec((d, tm), lambda i: (0, i)),
                pl.BlockSpec((1, r, d), lambda i: (i // tpg, 0, 2))]
    args = [x, y_t, mod3]
    if final_g is not None:
        in_specs.append(pl.BlockSpec((1, d), lambda i: (0, 0)))
        args.append(final_g.reshape(1, d))
    return pl.pallas_call(
        functools.partial(_res_kernel, final_norm=final_g is not None),
        name="peer_residual",
        out_shape=jax.ShapeDtypeStruct((t, d), F32),
        grid=(t // tm,),
        in_specs=in_specs,
        out_specs=pl.BlockSpec((tm, d), lambda i: (i, 0)),
        compiler_params=_cparams("arbitrary"),
    )(*args)


def _rope(x, cos, sin_signed):
    lane = _iota(x.shape, 1) % MOBA_HEAD_DIM
    half = MOBA_HEAD_DIM // 2
    width = x.shape[1]
    partner = jnp.where(lane < half, pltpu.roll(x, width - half, 1), pltpu.roll(x, half, 1))
    return x * cos + partner * sin_signed


def _ab_post_kernel(u_ref, v_ref, q_ref, k_ref, cos_ref, sin_ref, lng_ref, lnb_ref, ws_ref, bias_ref,
                    a_ref, vln_ref, qr_ref, kr_ref, km_ref, *, lc, n_chunks):
    u = _gelu(u_ref[...])
    v = _gelu(v_ref[...])
    mu = jnp.mean(v, axis=-1, keepdims=True)
    var = jnp.mean(jnp.square(v - mu), axis=-1, keepdims=True)
    vln = (v - mu) * lax.rsqrt(var + EPS) * lng_ref[...] + lnb_ref[...]
    vln_ref[...] = vln

    width = vln.shape[1]
    lane_group = _iota((lc, width), 1) // (width // GMLP_GROUPS)
    causal = _iota((lc, lc), 1) <= _iota((lc, lc), 0)
    w_tril = [jnp.where(causal, ws_ref[g], jnp.zeros((), BF16)) for g in range(GMLP_GROUPS)]
    for c in range(n_chunks):
        rows = slice(c * lc, (c + 1) * lc)
        vc = vln[rows].astype(BF16)
        mix = jnp.zeros((lc, width), F32)
        for g in range(GMLP_GROUPS):
            mix = mix + jnp.where(lane_group == g, _dot(w_tril[g], vc), 0.0)
        a_ref[rows, :] = (u[rows] * (mix + bias_ref[...])).astype(BF16)

    cos = cos_ref[...]
    sin = sin_ref[...]
    qr_ref[...] = _rope(q_ref[...], cos, sin).astype(BF16)
    kr = _rope(k_ref[...], cos, sin)
    kr_ref[...] = kr
    km_ref[0] = jnp.mean(kr, axis=0, keepdims=True)


def _ab_post(proj, cos, sin, ln_g, ln_b, ws, bias, *, tr, lc):
    t = proj.shape[0]
    w = ln_g.shape[0]
    n_tab = cos.shape[0] // tr
    n_tiles = t // tr
    col = lambda c: pl.BlockSpec((tr, w), lambda i: (i, c))
    tab = pl.BlockSpec((tr, w), lambda i: (i % n_tab, 0))
    row = pl.BlockSpec((tr, w), lambda i: (i, 0))
    return pl.pallas_call(
        functools.partial(_ab_post_kernel, lc=lc, n_chunks=tr // lc),
        name="gmlp_rope",
        out_shape=[jax.ShapeDtypeStruct((t, w), BF16),
                   jax.ShapeDtypeStruct((t, w), F32),
                   jax.ShapeDtypeStruct((t, w), BF16),
                   jax.ShapeDtypeStruct((t, w), F32),
                   jax.ShapeDtypeStruct((n_tiles, 1, w), F32)],
        grid=(n_tiles,),
        in_specs=[col(0), col(1), col(2), col(3), tab, tab,
                  pl.BlockSpec((1, w), lambda i: (0, 0)),
                  pl.BlockSpec((1, w), lambda i: (0, 0)),
                  pl.BlockSpec((GMLP_GROUPS, lc, lc), lambda i: (0, 0, 0)),
                  pl.BlockSpec((lc, w), lambda i: (0, 0))],
        out_specs=[row, row, row, row, pl.BlockSpec((1, 1, w), lambda i: (i, 0, 0))],
        compiler_params=_cparams("arbitrary"),
    )(proj, proj, proj, proj, cos, sin, ln_g.reshape(1, w), ln_b.reshape(1, w), ws, bias)


def _rope_tables(pos):
    half = MOBA_HEAD_DIM // 2
    inv = jnp.power(ROPE_THETA, -jnp.arange(half, dtype=F32) / half)
    ang = pos.astype(F32)[:, None] * inv[None, :]
    cos = jnp.cos(ang)
    sin = jnp.sin(ang)
    cos_full = jnp.tile(jnp.concatenate([cos, cos], axis=-1), (1, MOBA_HEADS))
    sin_signed = jnp.tile(jnp.concatenate([-sin, sin], axis=-1), (1, MOBA_HEADS))
    return cos_full, sin_signed


def _top_rows(gate, n_sel):
    idx = _iota(gate.shape, 0)
    sel = jnp.zeros(gate.shape, F32)
    cur = gate
    for _ in range(n_sel):
        m = jnp.max(cur, axis=0, keepdims=True)
        first = jnp.min(jnp.where(cur == m, idx, gate.shape[0]), axis=0, keepdims=True)
        one = jnp.logical_and(idx == first, m > NEG_INF)
        sel = jnp.where(one, 1.0, sel)
        cur = jnp.where(one, NEG_INF, cur)
    return sel


def _moba_prompt_kernel(q_ref, k_ref, vt_ref, km_ref, o_ref, sel_scr):
    qi = pl.program_id(2)
    q = q_ref[0, 0]
    tq = q.shape[0]
    nb = km_ref.shape[2]
    scale = MOBA_HEAD_DIM ** -0.5

    gate = _dot_nt(km_ref[0, 0], q)
    gate = jnp.where(_iota(gate.shape, 0) < qi, gate, NEG_INF)
    sel = _top_rows(gate, min(MOBA_TOPK, nb))
    for r in range(nb):
        sel_scr[r] = sel[r:r + 1, :]

    s = _dot_nt(k_ref[0, 0, qi], q) * scale
    s = jnp.where(_iota(s.shape, 0) <= _iota(s.shape, 1), s, NEG_INF)
    m0 = jnp.max(s, axis=0, keepdims=True)
    p = jnp.exp(s - m0)
    l0 = jnp.sum(p, axis=0, keepdims=True)
    acc0 = _dot(vt_ref[0, 0, qi], p.astype(BF16))

    def body(j, carry):
        m, l, acc = carry
        s = _dot_nt(k_ref[0, 0, j], q) * scale
        s = jnp.where(sel_scr[j] > 0.5, s, NEG_INF)
        m_new = jnp.maximum(m, jnp.max(s, axis=0, keepdims=True))
        corr = jnp.exp(m - m_new)
        p = jnp.exp(s - m_new)
        l = l * corr + jnp.sum(p, axis=0, keepdims=True)
        acc = acc * corr + _dot(vt_ref[0, 0, j], p.astype(BF16))
        return m_new, l, acc

    _, l, acc = lax.fori_loop(0, qi, body, (m0, l0, acc0))
    o_ref[0, 0] = (acc / l).astype(BF16)


def _moba_prompt(q_hm, k_blk, vt_blk, km_hm):
    bsz, nh, seq, d = q_hm.shape
    nb = k_blk.shape[2]
    tq = MOBA_BLOCK
    return pl.pallas_call(
        _moba_prompt_kernel,
        name="moba_prompt",
        out_shape=jax.ShapeDtypeStruct((bsz, nh, d, seq), BF16),
        grid=(bsz, nh, seq // tq),
        in_specs=[pl.BlockSpec((1, 1, tq, d), lambda b, h, i: (b, h, i, 0)),
                  pl.BlockSpec((1, 1, nb, MOBA_BLOCK, d), lambda b, h, i: (b, h, 0, 0, 0)),
                  pl.BlockSpec((1, 1, nb, d, MOBA_BLOCK), lambda b, h, i: (b, h, 0, 0, 0)),
                  pl.BlockSpec((1, 1, nb, d), lambda b, h, i: (b, h, 0, 0))],
        out_specs=pl.BlockSpec((1, 1, d, tq), lambda b, h, i: (b, h, 0, i)),
        scratch_shapes=[pltpu.VMEM((nb, 1, tq), F32)],
        compiler_params=_cparams("arbitrary", "arbitrary", "arbitrary"),
    )(q_hm, k_blk, vt_blk, km_hm)


def _moba_sample_kernel(pt_ref, q_ref, kn_ref, vn_ref, k0_ref, k1_ref, v0_ref, v1_ref, o_ref,
                        o_scr, m_scr, l_scr, ks_scr):
    j = pl.program_id(1)
    nblk = pl.num_programs(1)
    q = q_ref[0]
    rows, d = q.shape
    scale = MOBA_HEAD_DIM ** -0.5

    def partial_softmax(s, valid):
        s = jnp.where(valid, s * scale, NEG_INF)
        m = jnp.max(s, axis=-1, keepdims=True)
        p = jnp.exp(s - m)
        return m, jnp.sum(p, axis=-1, keepdims=True), p.astype(BF16)

    def same_head(shape):
        return (_iota(shape, 0) % MOBA_HEADS) == (_iota(shape, 1) % MOBA_HEADS)

    k0 = k0_ref[0]
    k1 = k1_ref[0]
    s = jnp.concatenate([_dot_nt(q, k0.astype(BF16)), _dot_nt(q, k1.astype(BF16))], axis=1)
    m, l, p = partial_softmax(s, same_head(s.shape))
    half = k0.shape[0]
    o = _dot(p[:, :half], v0_ref[0].astype(BF16)) + _dot(p[:, half:], v1_ref[0].astype(BF16))
    o_scr[j] = o
    m_scr[j] = jnp.broadcast_to(m, (rows, d))
    l_scr[j] = jnp.broadcast_to(l, (rows, d))
    ks_scr[j] = (jnp.sum(k0.reshape(PAGE_SIZE, MOBA_HEADS, d), axis=0)
                 + jnp.sum(k1.reshape(PAGE_SIZE, MOBA_HEADS, d), axis=0))

    @pl.when(j == nblk - 1)
    def _():
        nb = ks_scr.shape[0]
        kmean = (ks_scr[...] * (1.0 / MOBA_BLOCK)).astype(BF16).astype(F32)
        kmean_rows = jnp.concatenate([kmean] * (rows // MOBA_HEADS), axis=1)
        gate = jnp.sum(q.astype(F32)[None] * kmean_rows, axis=-1, keepdims=True)
        sel = _top_rows(jnp.broadcast_to(gate, (nb, rows, d)), min(MOBA_TOPK, nb)) > 0.5

        s_own = _dot_nt(q, kn_ref[0])
        valid = jnp.logical_and(same_head(s_own.shape),
                                _iota(s_own.shape, 1) // MOBA_HEADS <= _iota(s_own.shape, 0) // MOBA_HEADS)
        m_o, l_o, p_o = partial_softmax(s_own, valid)
        o_o = _dot(p_o, vn_ref[0])

        m_all = m_scr[...]
        m_top = jnp.maximum(m_o, jnp.max(jnp.where(sel, m_all, NEG_INF), axis=0))
        w = jnp.where(sel, jnp.exp(m_all - m_top[None]), 0.0)
        w_o = jnp.exp(m_o - m_top)
        l_tot = l_o * w_o + jnp.sum(w * l_scr[...], axis=0)
        o_tot = o_o * w_o + jnp.sum(w * o_scr[...], axis=0)
        o_ref[0] = o_tot / l_tot


def _moba_sample(page_table, q_rows, kn_rows, vn_rows, cache_k, cache_v):
    bsz, rows, d = q_rows.shape
    n_pages = page_table.shape[1]
    pages_per_blk = MOBA_BLOCK // PAGE_SIZE
    assert pages_per_blk == 2 and n_pages % pages_per_blk == 0
    nblk = n_pages // pages_per_blk
    page_rows = cache_k.shape[1]

    def page_spec(a):
        return pl.BlockSpec((1, page_rows, d), lambda b, j, pt: (pt[b * n_pages + 2 * j + a], 0, 0))

    seq_spec = pl.BlockSpec((1, rows, d), lambda b, j, pt: (b, 0, 0))
    grid_spec = pltpu.PrefetchScalarGridSpec(
        num_scalar_prefetch=1,
        grid=(bsz, nblk),
        in_specs=[seq_spec, seq_spec, seq_spec, page_spec(0), page_spec(1), page_spec(0), page_spec(1)],
        out_specs=seq_spec,
        scratch_shapes=[pltpu.VMEM((nblk, rows, d), F32), pltpu.VMEM((nblk, rows, d), F32),
                        pltpu.VMEM((nblk, rows, d), F32), pltpu.VMEM((nblk, MOBA_HEADS, d), F32)])
    return pl.pallas_call(
        _moba_sample_kernel,
        name="moba_sample",
        out_shape=jax.ShapeDtypeStruct((bsz, rows, d), F32),
        grid_spec=grid_spec,
        compiler_params=_cparams("arbitrary", "arbitrary"),
    )(page_table.reshape(-1), q_rows, kn_rows, vn_rows, cache_k, cache_k, cache_v, cache_v)


def _ssd_kernel(z_ref, xs_ref, bm_ref, cm_ref, dt_ref, conv0_ref, st0_ref, cw_ref, cb_ref, dtb_ref,
                alog_ref, dskip_ref, ng_ref, y_ref, stout_ref, convout_ref, st_scr, xbuf, yd_scr, *, lv):
    c = pl.program_id(1)
    last = pl.num_programs(1) - 1
    q = SSD_CHUNK
    hp = SSD_INNER
    gw = hp // SSD_GROUPS

    @pl.when(c == 0)
    def _():
        st_scr[...] = st0_ref[0]
        xbuf[0:8, :] = conv0_ref[0]

    def pad(x):
        if lv == q:
            return x
        return jnp.concatenate([x, jnp.zeros((q - lv, x.shape[1]), x.dtype)], axis=0)

    xbuf[8:8 + q, 0:hp] = pad(xs_ref[...])
    xbuf[8:8 + q, hp:hp + SSD_BC] = pad(bm_ref[...])
    xbuf[8:8 + q, hp + SSD_BC:] = pad(cm_ref[...])
    first = 8 - (SSD_CONV - 1)
    conv = cb_ref[...] + xbuf[first:first + q, :] * cw_ref[0:1, :]
    for tap in range(1, SSD_CONV):
        conv = conv + xbuf[first + tap:first + tap + q, :] * cw_ref[tap:tap + 1, :]

    @pl.when(c == last)
    def _():
        convout_ref[0] = xbuf[8 + lv - (SSD_CONV - 1):8 + lv, :]

    xbuf[0:8, :] = xbuf[q:q + 8, :]

    xbc = jax.nn.silu(conv)
    xs = xbc[:, :hp]
    bmat = xbc[:, hp:hp + SSD_BC]
    cmat = xbc[:, hp + SSD_BC:]

    dt = jax.nn.softplus(pad(dt_ref[...]) + dtb_ref[...])
    if lv != q:
        dt = jnp.where(_iota(dt.shape, 0) < lv, dt, 0.0)
    adt = dt * (-jnp.exp(alog_ref[...]))
    tri = (_iota((q, q), 1) <= _iota((q, q), 0)).astype(BF16)
    hi, mid, lo = _split3(adt)
    a_cum = _dot(tri, hi) + _dot(tri, mid) + _dot(tri, lo)
    a_cum_t = a_cum.T

    head_of_col = _iota((LANES, hp), 1) // SSD_HEAD_DIM
    e_cols = (head_of_col == _iota((LANES, hp), 0)).astype(BF16)
    e_time = (_iota((LANES, SSD_HEADS * q), 1) // q == _iota((LANES, SSD_HEADS * q), 0)).astype(BF16)
    dt_full = _dot_exact_rhs01(dt, e_cols)
    a_full = _dot_exact_rhs01(a_cum, e_cols)
    a_col = _dot_exact_rhs01(a_cum, e_time)
    a_last = a_full[q - 1:q, :]

    xdt = xs * dt_full
    xdt_b = xdt.astype(BF16)
    xdt_end = (xdt * jnp.exp(a_last - a_full)).astype(BF16)
    grow = jnp.exp(a_full)
    carry = jnp.exp(a_last)
    causal = _iota((q, q), 1) <= _iota((q, q), 0)
    low_half = _iota((q, LANES), 1) < SSD_HEAD_DIM
    heads_per_group = SSD_HEADS // SSD_GROUPS

    for g in range(SSD_GROUPS):
        cols = slice(g * gw, (g + 1) * gw)
        cc = cmat[:, g * SSD_STATE:(g + 1) * SSD_STATE].astype(BF16)
        bc = bmat[:, g * SSD_STATE:(g + 1) * SSD_STATE]
        cb = _dot_nt(cc, bc.astype(BF16))
        st_g = st_scr[:, cols]
        y_off = _dot(cc, st_g.astype(BF16)) * grow[:, cols]
        st_scr[:, cols] = st_g * carry[:, cols] + _dot(bc.T.astype(BF16), xdt_end[:, cols])
        for pair in range(heads_per_group // 2):
            h0 = g * heads_per_group + 2 * pair
            pcols = slice(h0 * SSD_HEAD_DIM, (h0 + 2) * SSD_HEAD_DIM)
            outs = []
            for h in (h0, h0 + 1):
                seg = a_col[:, h * q:(h + 1) * q] - a_cum_t[h:h + 1, :]
                decay = jnp.exp(jnp.where(causal, seg, NEG_INF))
                outs.append(_dot((cb * decay).astype(BF16), xdt_b[:, pcols]))
            yd_scr[:, pcols] = jnp.where(low_half, outs[0], outs[1]) + y_off[:, pcols.start - g * gw:pcols.stop - g * gw]

    y = yd_scr[...] + dskip_ref[...] * xs
    y = y * jax.nn.silu(pad(z_ref[...]))
    y = y * lax.rsqrt(jnp.mean(y * y, axis=-1, keepdims=True) + EPS) * ng_ref[...]
    y_ref[...] = y[:lv].astype(y_ref.dtype)

    @pl.when(c == last)
    def _():
        stout_ref[0] = st_scr[...]


def _ssd_scan(proj, conv0p, st0_t, conv_w, conv_b, dt_bias, a_log, dskip_full, norm_g, *, bsz, seq_len):
    lv = min(seq_len, SSD_CHUNK)
    nc = seq_len // lv
    hp = SSD_INNER
    t = proj.shape[0]
    blk = lambda w, cidx: pl.BlockSpec((lv, w), lambda b, c: (b * nc + c, cidx))
    full = lambda shape: pl.BlockSpec(shape, lambda b, c: (0,) * len(shape))
    per_seq = lambda shape: pl.BlockSpec((1,) + shape, lambda b, c: (b,) + (0,) * len(shape))
    return pl.pallas_call(
        functools.partial(_ssd_kernel, lv=lv),
        name="ssd_scan",
        out_shape=[jax.ShapeDtypeStruct((t, hp), BF16 if lv % 16 == 0 else F32),
                   jax.ShapeDtypeStruct((bsz, SSD_STATE, hp), F32),
                   jax.ShapeDtypeStruct((bsz, SSD_CONV - 1, SSD_CONV_DIM), F32)],
        grid=(bsz, nc),
        in_specs=[blk(hp, 0), blk(hp, 1), blk(SSD_BC, 2 * hp // SSD_BC), blk(SSD_BC, 2 * hp // SSD_BC + 1),
                  blk(LANES, (2 * hp + 2 * SSD_BC) // LANES),
                  per_seq((8, SSD_CONV_DIM)), per_seq((SSD_STATE, hp)),
                  full((SSD_CONV, SSD_CONV_DIM)), full((1, SSD_CONV_DIM)), full((1, LANES)), full((1, LANES)),
                  full((1, hp)), full((1, hp))],
        out_specs=[pl.BlockSpec((lv, hp), lambda b, c: (b * nc + c, 0)),
                   per_seq((SSD_STATE, hp)), per_seq((SSD_CONV - 1, SSD_CONV_DIM))],
        scratch_shapes=[pltpu.VMEM((SSD_STATE, hp), F32), pltpu.VMEM((SSD_CHUNK + 8, SSD_CONV_DIM), F32),
                        pltpu.VMEM((SSD_CHUNK, hp), F32)],
        compiler_params=_cparams("arbitrary", "arbitrary"),
    )(proj, proj, proj, proj, proj, conv0p, st0_t, conv_w, conv_b.reshape(1, -1), dt_bias, a_log,
      dskip_full, norm_g.reshape(1, -1))


def _top_ranked(scores, k):
    rows, n = scores.shape
    idx = _iota((rows, n), 0)
    slot = _iota((k, n), 0)

    def body(i, carry):
        cur, rank, vals = carry
        m = jnp.max(cur, axis=0, keepdims=True)
        first = jnp.min(jnp.where(cur == m, idx, rows), axis=0, keepdims=True)
        one = idx == first
        rank = jnp.where(one, i.astype(F32), rank)
        cur = jnp.where(one, NEG_INF, cur)
        vals = jnp.where(slot == i, m, vals)
        return cur, rank, vals

    init = (scores, jnp.full((rows, n), float(k), F32), jnp.zeros((k, n), F32))
    _, rank, vals = lax.fori_loop(0, k, body, init)
    return vals, rank


def _peer_route_kernel(q_ref, sk_ref, alpha_ref, count_ref, beta_ref, rank_ref):
    k = PEER_TOPK
    hd = sk_ref.shape[3]
    q = q_ref[...]
    s1 = _dot_nt(sk_ref[0, 0], q[:, :hd])
    s2 = _dot_nt(sk_ref[0, 1], q[:, hd:])
    v1, rank1 = _top_ranked(s1, k)
    v2, rank2 = _top_ranked(s2, k)

    n = s1.shape[1]
    cand = jnp.concatenate([v1[a:a + 1, :] + v2 for a in range(k)], axis=0)
    flat = _iota(cand.shape, 0)

    def body(_, carry):
        cur, sel = carry
        m = jnp.max(cur, axis=0, keepdims=True)
        first = jnp.min(jnp.where(cur == m, flat, k * k), axis=0, keepdims=True)
        one = flat == first
        return jnp.where(one, NEG_INF, cur), jnp.where(one, 1.0, sel)

    _, sel = lax.fori_loop(0, k, body, (cand, jnp.zeros(cand.shape, F32)))
    z = jnp.sum(sel * jnp.exp(cand - cand[0:1, :]), axis=0, keepdims=True)

    count = jnp.zeros(rank1.shape, F32)
    for a in range(k):
        n_a = jnp.sum(sel[a * k:(a + 1) * k, :], axis=0, keepdims=True)
        count = jnp.where(rank1 == float(a), n_a, count)
    alpha_ref[0] = jnp.exp(s1 - v1[0:1, :]) / z
    count_ref[0] = count
    beta_ref[0] = jnp.exp(s2 - v2[0:1, :])
    rank_ref[0] = rank2


def _peer_route(q, subkeys, *, tt):
    t = q.shape[0]
    nh, _, nk, hd = subkeys.shape
    out = jax.ShapeDtypeStruct((nh, nk, t), F32)
    ospec = pl.BlockSpec((1, nk, tt), lambda i, h: (h, 0, i))
    return pl.pallas_call(
        _peer_route_kernel,
        name="peer_route",
        out_shape=[out, out, out, out],
        grid=(t // tt, nh),
        in_specs=[pl.BlockSpec((tt, 2 * hd), lambda i, h: (i, h)),
                  pl.BlockSpec((1, 2, nk, hd), lambda i, h: (h, 0, 0, 0))],
        out_specs=[ospec, ospec, ospec, ospec],
        compiler_params=_cparams("arbitrary", "arbitrary"),
    )(q, subkeys)


def _peer_dense_kernel(h_ref, u_ref, vt_ref, alpha_ref, count_ref, beta_ref, rank_ref, o_ref, aw_scr, *, ib):
    e = pl.program_id(1)
    nh, nk, tt = beta_ref.shape

    @pl.when(e == 0)
    def _():
        o_ref[...] = jnp.zeros(o_ref.shape, F32)

    s_t = _dot_nt(u_ref[...], h_ref[...])
    i0 = pl.multiple_of(e * ib, ib)
    for lt in range(tt // LANES):
        lanes = slice(lt * LANES, (lt + 1) * LANES)
        counts = [count_ref[h, pl.ds(i0, ib), lanes] for h in range(nh)]
        alphas = [alpha_ref[h, pl.ds(i0, ib), lanes] for h in range(nh)]
        for ii in range(ib):
            w = jnp.zeros((nk, LANES), F32)
            for h in range(nh):
                hit = rank_ref[h, :, lanes] < counts[h][ii:ii + 1, :]
                w = w + jnp.where(hit, beta_ref[h, :, lanes], 0.0) * alphas[h][ii:ii + 1, :]
            rows = slice(ii * nk, (ii + 1) * nk)
            aw_scr[rows, lanes] = (w * _gelu(s_t[rows, lanes])).astype(BF16)
    o_ref[...] += _dot(vt_ref[...], aw_scr[...])


def _peer_dense(h, u_tab, vt_tab, route, *, tt, ib):
    t, d = h.shape
    alpha, count, beta, rank = route
    nh, nk, _ = alpha.shape
    te = ib * nk
    rspec = pl.BlockSpec((nh, nk, tt), lambda i, e: (0, 0, i))
    return pl.pallas_call(
        functools.partial(_peer_dense_kernel, ib=ib),
        name="peer_dense",
        out_shape=jax.ShapeDtypeStruct((d, t), F32),
        grid=(t // tt, nk // ib),
        in_specs=[pl.BlockSpec((tt, d), lambda i, e: (i, 0)),
                  pl.BlockSpec((te, d), lambda i, e: (e, 0)),
                  pl.BlockSpec((d, te), lambda i, e: (0, e)),
                  rspec, rspec, rspec, rspec],
        out_specs=pl.BlockSpec((d, tt), lambda i, e: (0, i)),
        scratch_shapes=[pltpu.VMEM((te, tt), BF16)],
        compiler_params=_cparams("arbitrary", "arbitrary"),
    )(h, u_tab, vt_tab, alpha, count, beta, rank)


def _prep_weights(norm_mix_g, mod_mix_w, mod_mix_b, norm_ffn_g, mod_ffn_w, mod_ffn_b,
                  w_in_ab, w_out_ab, gmlp_ln_g, gmlp_ln_b, gmlp_ws, gmlp_bs,
                  ssd_w_in, ssd_conv_w, ssd_conv_b, ssd_dt_bias, ssd_a_log, ssd_d, ssd_norm_g, ssd_w_out,
                  peer_wq, peer_subkeys, peer_u, peer_v, final_norm_g):
    depth = norm_mix_g.shape[0]
    d = norm_mix_g.shape[1]
    p = dict(depth=depth, norm_mix_g=norm_mix_g, norm_ffn_g=norm_ffn_g, final_norm_g=final_norm_g,
             gmlp_ln_g=gmlp_ln_g, gmlp_ln_b=gmlp_ln_b, gmlp_ws=gmlp_ws, gmlp_bs=gmlp_bs,
             ssd_conv_w=ssd_conv_w, ssd_conv_b=ssd_conv_b, ssd_norm_g=ssd_norm_g)
    mod_w, mod_b = [], []
    for i in range(depth):
        mod_w += [mod_mix_w[i], mod_ffn_w[i]]
        mod_b += [mod_mix_b[i], mod_ffn_b[i]]
    p["mod_w"] = jnp.stack(mod_w).astype(BF16)
    p["mod_b"] = jnp.stack(mod_b)[:, None, :]
    p["w_in_ab"] = w_in_ab.astype(BF16)
    p["w_out_ab"] = w_out_ab.astype(BF16)
    n_ssd = ssd_w_in.shape[0]
    z_w = ssd_w_in[:, :, :SSD_INNER]
    xbc_w = ssd_w_in[:, :, SSD_INNER:SSD_INNER + SSD_CONV_DIM]
    dt_w = ssd_w_in[:, :, SSD_INNER + SSD_CONV_DIM:]
    used = 2 * SSD_INNER + 2 * SSD_BC + LANES
    total = -(-used // 1792) * 1792
    p["ssd_w_in"] = jnp.concatenate(
        [z_w, xbc_w, dt_w, jnp.zeros((n_ssd, d, total - used + LANES - SSD_HEADS), F32)], axis=-1).astype(BF16)
    pad_heads = lambda a, v: jnp.concatenate([a, jnp.full((n_ssd, LANES - SSD_HEADS), v, F32)], axis=-1)[:, None, :]
    p["ssd_dt_bias"] = pad_heads(ssd_dt_bias, 0.0)
    p["ssd_a_log"] = pad_heads(ssd_a_log, 0.0)
    p["ssd_dskip"] = jnp.repeat(ssd_d, SSD_HEAD_DIM, axis=-1)[:, None, :]
    p["ssd_w_out"] = ssd_w_out.astype(BF16)
    p["peer_wq"] = peer_wq.astype(BF16)
    p["peer_subkeys"] = peer_subkeys.astype(BF16)
    p["peer_u"] = peer_u.astype(BF16)
    p["peer_vt"] = jnp.swapaxes(peer_v, 1, 2).astype(BF16)
    return p


def _peer_layer(x, mod, seq_len, p, i, *, tm, final):
    q, h = _adaln_matmul(x, p["norm_ffn_g"][i], mod, seq_len, p["peer_wq"][i], tm=tm,
                         tn=p["peer_wq"].shape[2] // 2, out_dtype=BF16, emit_h=True)
    nk = p["peer_subkeys"].shape[3]
    route = _peer_route(q, p["peer_subkeys"][i], tt=min(256, tm))
    y_t = _peer_dense(h, p["peer_u"][i], p["peer_vt"][i], route, tt=min(256, tm), ib=8)
    return _residual_t(x, y_t, mod, seq_len, tm=tm, final_g=p["final_norm_g"] if final else None)


def _ab_layer(x, mod, bsz, seq_len, pos, past, p, i, j, *, tm):
    t, d = x.shape
    w = GMLP_GROUPS * (d // 16)
    proj = _adaln_matmul(x, p["norm_mix_g"][i], mod, seq_len, p["w_in_ab"][j], tm=tm,
                         tn=p["w_in_ab"].shape[2] // 2)
    cos, sin = _rope_tables(pos)
    lc = min(seq_len, GMLP_CHUNK)
    ws = p["gmlp_ws"][j][:, :lc, :lc]
    bias = jnp.repeat(p["gmlp_bs"][j][:, :lc].T, w // GMLP_GROUPS, axis=1)
    if seq_len < GMLP_CHUNK:
        reps = tm // seq_len
        ws = jnp.where(jnp.tril(jnp.ones((lc, lc), bool)), ws, 0.0)
        eye = jnp.eye(reps, dtype=F32)
        ws = jnp.einsum("ab,gts->gatbs", eye, ws).reshape(GMLP_GROUPS, tm, tm)
        bias = jnp.tile(bias, (reps, 1))
        cos = jnp.tile(cos, (reps, 1))
        sin = jnp.tile(sin, (reps, 1))
        tr, lc_eff = tm, tm
    else:
        tr, lc_eff = MOBA_BLOCK, lc
    a_out, vln, q_rot, k_rot, kmean = _ab_post(proj, cos, sin, p["gmlp_ln_g"][j], p["gmlp_ln_b"][j],
                                               ws.astype(BF16), bias, tr=tr, lc=lc_eff)
    v_new = proj[:, 2 * w + 2 * w:2 * w + 3 * w]
    nh, hd = MOBA_HEADS, MOBA_HEAD_DIM
    if past is None:
        nb = seq_len // MOBA_BLOCK
        q_hm = q_rot.reshape(bsz, seq_len, nh, hd).transpose(0, 2, 1, 3)
        k_blk = k_rot.astype(BF16).reshape(bsz, nb, MOBA_BLOCK, nh, hd).transpose(0, 3, 1, 2, 4)
        vt_blk = v_new.astype(BF16).reshape(bsz, nb, MOBA_BLOCK, nh, hd).transpose(0, 3, 1, 4, 2)
        km_hm = kmean.reshape(bsz, nb, nh, hd).transpose(0, 2, 1, 3).astype(BF16)
        o_t = _moba_prompt(q_hm, k_blk, vt_blk, km_hm)
        b_out = o_t.transpose(0, 3, 1, 2).reshape(t, w)
    else:
        page_table, cache_k, cache_v = past
        rows = seq_len * nh
        q_rows = q_rot.reshape(bsz, rows, hd)
        kn_rows = k_rot.astype(BF16).reshape(bsz, rows, hd)
        vn_rows = v_new.astype(BF16).reshape(bsz, rows, hd)
        n_pool = cache_k.shape[0]
        b_out = _moba_sample(page_table, q_rows, kn_rows, vn_rows,
                             cache_k.reshape(n_pool, PAGE_SIZE * nh, hd),
                             cache_v.reshape(n_pool, PAGE_SIZE * nh, hd))
        b_out = b_out.reshape(t, w).astype(BF16)
    x = _proj_residual([a_out, b_out], p["w_out_ab"][j], x, mod, seq_len, tm=tm)
    lcv = min(seq_len, GMLP_CHUNK)
    gv = vln.reshape(bsz, seq_len, w)[:, seq_len - lcv:]
    return (x, k_rot.reshape(bsz, seq_len, nh, hd), v_new.reshape(bsz, seq_len, nh, hd), gv)


def _ssd_layer(x, mod, bsz, seq_len, ssm0, conv0, p, i, j, *, tm):
    t, d = x.shape
    proj = _adaln_matmul(x, p["norm_mix_g"][i], mod, seq_len, p["ssd_w_in"][j], tm=tm, tn=1792)
    if conv0 is None:
        conv0 = jnp.zeros((bsz, SSD_CONV - 1, SSD_CONV_DIM), F32)
    if ssm0 is None:
        st0_t = jnp.zeros((bsz, SSD_STATE, SSD_INNER), F32)
    else:
        st0_t = ssm0.reshape(bsz, SSD_INNER, SSD_STATE).transpose(0, 2, 1)
    conv0p = jnp.pad(conv0, ((0, 0), (8 - (SSD_CONV - 1), 0), (0, 0)))
    y, st_t, conv_new = _ssd_scan(proj, conv0p, st0_t, p["ssd_conv_w"][j], p["ssd_conv_b"][j],
                                  p["ssd_dt_bias"][j], p["ssd_a_log"][j], p["ssd_dskip"][j],
                                  p["ssd_norm_g"][j], bsz=bsz, seq_len=seq_len)
    x = _proj_residual([y], p["ssd_w_out"][j], x, mod, seq_len, tm=tm)
    ssm_new = st_t.transpose(0, 2, 1).reshape(bsz, SSD_HEADS, SSD_HEAD_DIM, SSD_STATE)
    return x, ssm_new, conv_new


def _run_trunk(x, c, pos0, past_kv, ssm_past, conv_past, p):
    bsz, seq_len, d = x.shape
    t = bsz * seq_len
    tm = 512 if t % 512 == 0 and seq_len % 512 == 0 else 256
    assert t % tm == 0
    x = x.reshape(t, d)
    pos = pos0 + jnp.arange(seq_len, dtype=jnp.int32)
    mods = _mod_all(c, p["mod_w"], p["mod_b"])
    depth = p["depth"]
    k_rows, v_rows, gv_rows, ssm_new, conv_new = [], [], [], [], []
    for i in range(depth):
        j = i // 2
        if i % 2 == 0:
            past = None if past_kv is None else (past_kv[0], past_kv[1][j], past_kv[2][j])
            x, k_new, v_new, gv = _ab_layer(x, mods[2 * i], bsz, seq_len, pos, past, p, i, j, tm=tm)
            k_rows.append(k_new)
            v_rows.append(v_new)
            gv_rows.append(gv)
        else:
            x, s_new, cv_new = _ssd_layer(x, mods[2 * i], bsz, seq_len,
                                          None if ssm_past is None else ssm_past[j],
                                          None if conv_past is None else conv_past[j], p, i, j, tm=tm)
            ssm_new.append(s_new)
            conv_new.append(cv_new)
        x = _peer_layer(x, mods[2 * i + 1], seq_len, p, i, tm=tm, final=(i == depth - 1))
    y = x.reshape(bsz, seq_len, d)
    return y, jnp.stack(k_rows), jnp.stack(v_rows), jnp.stack(gv_rows), jnp.stack(ssm_new), jnp.stack(conv_new)


def kernel(x_prompt, x_sample, cache_k, cache_v, page_table, state_ssm, state_conv, c_prompt, c_sample,
           norm_mix_g, mod_mix_w, mod_mix_b, norm_ffn_g, mod_ffn_w, mod_ffn_b,
           w_in_ab, w_out_ab, gmlp_ln_g, gmlp_ln_b, gmlp_ws, gmlp_bs,
           ssd_w_in, ssd_conv_w, ssd_conv_b, ssd_dt_bias, ssd_a_log, ssd_d, ssd_norm_g, ssd_w_out,
           peer_wq, peer_subkeys, peer_u, peer_v, final_norm_g):
    p = _prep_weights(norm_mix_g, mod_mix_w, mod_mix_b, norm_ffn_g, mod_ffn_w, mod_ffn_b,
                      w_in_ab, w_out_ab, gmlp_ln_g, gmlp_ln_b, gmlp_ws, gmlp_bs,
                      ssd_w_in, ssd_conv_w, ssd_conv_b, ssd_dt_bias, ssd_a_log, ssd_d, ssd_norm_g, ssd_w_out,
                      peer_wq, peer_subkeys, peer_u, peer_v, final_norm_g)
    prompt = _run_trunk(x_prompt, c_prompt, 0, None, None, None, p)
    past_len = page_table.shape[1] * PAGE_SIZE
    sample = _run_trunk(x_sample, c_sample, past_len, (page_table, cache_k, cache_v),
                        state_ssm, state_conv, p)
    return (prompt[0], sample[0]) + prompt[1:] + sample[1:]
```

```python
import functools
import math

import jax
import jax.numpy as jnp
from jax import lax
from jax.experimental import pallas as pl
from jax.experimental.pallas import tpu as pltpu

F32 = jnp.float32
BF16 = jnp.bfloat16
NEG_INF = float("-inf")

EPS = 1e-6
LANES = 128
GMLP_GROUPS = 8
GMLP_CHUNK = 128
MOBA_HEADS = 8
MOBA_HEAD_DIM = 64
MOBA_BLOCK = 256
MOBA_TOPK = 3
PAGE_SIZE = 128
ROPE_THETA = 10000.0
SSD_HEADS = 32
SSD_HEAD_DIM = 64
SSD_GROUPS = 4
SSD_STATE = 128
SSD_CONV = 4
SSD_CHUNK = 128
SSD_INNER = SSD_HEADS * SSD_HEAD_DIM
SSD_BC = SSD_GROUPS * SSD_STATE
SSD_CONV_DIM = SSD_INNER + 2 * SSD_BC
PEER_HEADS = 8
PEER_TOPK = 16
VMEM_LIMIT_BYTES = 56 * 1024 * 1024


def _cparams(*sem):
    return pltpu.CompilerParams(dimension_semantics=sem, vmem_limit_bytes=VMEM_LIMIT_BYTES)


def _dot(a, b):
    return jnp.dot(a, b, preferred_element_type=F32)


def _dot_nt(a, b):
    return lax.dot_general(a, b, (((1,), (1,)), ((), ())), preferred_element_type=F32)


def _iota(shape, dim):
    return lax.broadcasted_iota(jnp.int32, shape, dim)


def _gelu(x):
    return 0.5 * x * (1.0 + lax.erf(x * (2.0 ** -0.5)))


def _split3(x):
    hi = x.astype(BF16)
    r = x - hi.astype(F32)
    mid = r.astype(BF16)
    lo = (r - mid.astype(F32)).astype(BF16)
    return hi, mid, lo


def _dot_exact_rhs01(x, e):
    hi, mid, lo = _split3(x)
    return _dot(hi, e) + _dot(mid, e) + _dot(lo, e)


def _mod_kernel(c_ref, w_ref, b_ref, o_ref):
    o_ref[0] = _dot(c_ref[...].astype(BF16), w_ref[0]) + b_ref[0]


def _mod_all(c, w4, b4):
    bsz, d = c.shape
    rows = -(-bsz // 16) * 16
    cp = jnp.pad(c, ((0, rows - bsz), (0, 0)))
    n = w4.shape[0]
    out = pl.pallas_call(
        _mod_kernel,
        name="adaln_mod",
        out_shape=jax.ShapeDtypeStruct((n, rows, 3 * d), F32),
        grid=(n, 3),
        in_specs=[pl.BlockSpec((rows, d), lambda i, j: (0, 0)),
                  pl.BlockSpec((1, d, d), lambda i, j: (i, 0, j)),
                  pl.BlockSpec((1, 1, d), lambda i, j: (i, 0, j))],
        out_specs=pl.BlockSpec((1, rows, d), lambda i, j: (i, 0, j)),
        compiler_params=_cparams("arbitrary", "arbitrary"),
    )(cp, w4, b4)
    return out[:, :bsz]


def _mod_rows(mod, seq_len, tm):
    bsz, w = mod.shape
    if seq_len % tm == 0:
        return mod[:, None, :], seq_len // tm
    assert tm % seq_len == 0 and (bsz * seq_len) % tm == 0
    return jnp.repeat(mod, seq_len, axis=0).reshape(bsz * seq_len // tm, tm, w), 1


def _adaln_mm_kernel(x_ref, g_ref, sh_ref, sc_ref, w_ref, o_ref, *rest, emit_h):
    if emit_h:
        h_ref, h_scr = rest
    else:
        (h_scr,) = rest

    @pl.when(pl.program_id(1) == 0)
    def _():
        x = x_ref[...]
        y = x * lax.rsqrt(jnp.mean(x * x, axis=-1, keepdims=True) + EPS)
        y = y * g_ref[...]
        h = (y * (1.0 + sc_ref[0]) + sh_ref[0]).astype(BF16)
        h_scr[...] = h
        if emit_h:
            h_ref[...] = h

    o_ref[...] = _dot(h_scr[...], w_ref[...]).astype(o_ref.dtype)


def _adaln_matmul(x, g, mod, seq_len, w, *, tm, tn, out_dtype=F32, emit_h=False):
    t, d = x.shape
    n = w.shape[1]
    assert t % tm == 0 and n % tn == 0
    mod3, tpg = _mod_rows(mod, seq_len, tm)
    r = mod3.shape[1]
    out_shape = [jax.ShapeDtypeStruct((t, n), out_dtype)]
    out_specs = [pl.BlockSpec((tm, tn), lambda i, j: (i, j))]
    if emit_h:
        out_shape.append(jax.ShapeDtypeStruct((t, d), BF16))
        out_specs.append(pl.BlockSpec((tm, d), lambda i, j: (i, 0)))
    res = pl.pallas_call(
        functools.partial(_adaln_mm_kernel, emit_h=emit_h),
        name="adaln_matmul",
        out_shape=out_shape,
        grid=(t // tm, n // tn),
        in_specs=[pl.BlockSpec((tm, d), lambda i, j: (i, 0)),
                  pl.BlockSpec((1, d), lambda i, j: (0, 0)),
                  pl.BlockSpec((1, r, d), lambda i, j: (i // tpg, 0, 0)),
                  pl.BlockSpec((1, r, d), lambda i, j: (i // tpg, 0, 1)),
                  pl.BlockSpec((d, tn), lambda i, j: (0, j))],
        out_specs=out_specs,
        scratch_shapes=[pltpu.VMEM((tm, d), BF16)],
        compiler_params=_cparams("arbitrary", "arbitrary"),
    )(x, g.reshape(1, d), mod3, mod3, w)
    return res if emit_h else res[0]


def _proj_res_kernel(*refs, ks):
    lhs = refs[:len(ks)]
    w_ref, x_ref, gate_ref, o_ref = refs[len(ks):]
    acc = None
    off = 0
    for l_ref, k in zip(lhs, ks):
        part = _dot(l_ref[...].astype(BF16), w_ref[off:off + k, :])
        acc = part if acc is None else acc + part
        off += k
    o_ref[...] = x_ref[...] + gate_ref[0] * acc


def _proj_residual(lhs, w, x, mod, seq_len, *, tm):
    t, d = x.shape
    ks = tuple(l.shape[1] for l in lhs)
    mod3, tpg = _mod_rows(mod, seq_len, tm)
    r = mod3.shape[1]
    in_specs = [pl.BlockSpec((tm, k), lambda i: (i, 0)) for k in ks]
    in_specs += [pl.BlockSpec((sum(ks), d), lambda i: (0, 0)),
                 pl.BlockSpec((tm, d), lambda i: (i, 0)),
                 pl.BlockSpec((1, r, d), lambda i: (i // tpg, 0, 2))]
    return pl.pallas_call(
        functools.partial(_proj_res_kernel, ks=ks),
        name="proj_residual",
        out_shape=jax.ShapeDtypeStruct((t, d), F32),
        grid=(t // tm,),
        in_specs=in_specs,
        out_specs=pl.BlockSpec((tm, d), lambda i: (i, 0)),
        compiler_params=_cparams("arbitrary"),
    )(*lhs, w, x, mod3)


def _res_kernel(x_ref, yt_ref, gate_ref, *rest, final_norm):
    x2 = x_ref[...] + gate_ref[0] * yt_ref[...].T
    if final_norm:
        g_ref, o_ref = rest
        o_ref[...] = x2 * lax.rsqrt(jnp.mean(x2 * x2, axis=-1, keepdims=True) + EPS) * g_ref[...]
    else:
        (o_ref,) = rest
        o_ref[...] = x2


def _residual_t(x, y_t, mod, seq_len, *, tm, final_g=None):
    t, d = x.shape
    mod3, tpg = _mod_rows(mod, seq_len, tm)
    r = mod3.shape[1]
    in_specs = [pl.BlockSpec((tm, d), lambda i: (i, 0)),
                pl.BlockSpec((d, tm), lambda i: (0, i)),
                pl.BlockSpec((1, r, d), lambda i: (i // tpg, 0, 2))]
    args = [x, y_t, mod3]
    if final_g is not None:
        in_specs.append(pl.BlockSpec((1, d), lambda i: (0, 0)))
        args.append(final_g.reshape(1, d))
    return pl.pallas_call(
        functools.partial(_res_kernel, final_norm=final_g is not None),
        name="peer_residual",
        out_shape=jax.ShapeDtypeStruct((t, d), F32),
        grid=(t // tm,),
        in_specs=in_specs,
        out_specs=pl.BlockSpec((tm, d), lambda i: (i, 0)),
        compiler_params=_cparams("arbitrary"),
    )(*args)


def _rope(x, cos, sin_signed):
    lane = _iota(x.shape, 1) % MOBA_HEAD_DIM
    half = MOBA_HEAD_DIM // 2
    width = x.shape[1]
    partner = jnp.where(lane < half, pltpu.roll(x, width - half, 1), pltpu.roll(x, half, 1))
    return x * cos + partner * sin_signed


def _ab_post_kernel(u_ref, v_ref, q_ref, k_ref, cos_ref, sin_ref, lng_ref, lnb_ref, ws_ref, bias_ref,
                    a_ref, vln_ref, qr_ref, kr_ref, km_ref, *, lc, n_chunks):
    u = _gelu(u_ref[...])
    v = _gelu(v_ref[...])
    mu = jnp.mean(v, axis=-1, keepdims=True)
    var = jnp.mean(jnp.square(v - mu), axis=-1, keepdims=True)
    vln = (v - mu) * lax.rsqrt(var + EPS) * lng_ref[...] + lnb_ref[...]
    vln_ref[...] = vln

    width = vln.shape[1]
    lane_group = _iota((lc, width), 1) // (width // GMLP_GROUPS)
    causal = _iota((lc, lc), 1) <= _iota((lc, lc), 0)
    w_tril = [jnp.where(causal, ws_ref[g], jnp.zeros((), BF16)) for g in range(GMLP_GROUPS)]
    for c in range(n_chunks):
        rows = slice(c * lc, (c + 1) * lc)
        vc = vln[rows].astype(BF16)
        mix = jnp.zeros((lc, width), F32)
        for g in range(GMLP_GROUPS):
            mix = mix + jnp.where(lane_group == g, _dot(w_tril[g], vc), 0.0)
        a_ref[rows, :] = (u[rows] * (mix + bias_ref[...])).astype(BF16)

    cos = cos_ref[...]
    sin = sin_ref[...]
    qr_ref[...] = _rope(q_ref[...], cos, sin).astype(BF16)
    kr = _rope(k_ref[...], cos, sin)
    kr_ref[...] = kr
    km_ref[0] = jnp.mean(kr, axis=0, keepdims=True)


def _ab_post(proj, cos, sin, ln_g, ln_b, ws, bias, *, tr, lc):
    t = proj.shape[0]
    w = ln_g.shape[0]
    n_tab = cos.shape[0] // tr
    n_tiles = t // tr
    col = lambda c: pl.BlockSpec((tr, w), lambda i: (i, c))
    tab = pl.BlockSpec((tr, w), lambda i: (i % n_tab, 0))
    row = pl.BlockSpec((tr, w), lambda i: (i, 0))
    return pl.pallas_call(
        functools.partial(_ab_post_kernel, lc=lc, n_chunks=tr // lc),
        name="gmlp_rope",
        out_shape=[jax.ShapeDtypeStruct((t, w), BF16),
                   jax.ShapeDtypeStruct((t, w), F32),
                   jax.ShapeDtypeStruct((t, w), BF16),
                   jax.ShapeDtypeStruct((t, w), F32),
                   jax.ShapeDtypeStruct((n_tiles, 1, w), F32)],
        grid=(n_tiles,),
        in_specs=[col(0), col(1), col(2), col(3), tab, tab,
                  pl.BlockSpec((1, w), lambda i: (0, 0)),
                  pl.BlockSpec((1, w), lambda i: (0, 0)),
                  pl.BlockSpec((GMLP_GROUPS, lc, lc), lambda i: (0, 0, 0)),
                  pl.BlockSpec((lc, w), lambda i: (0, 0))],
        out_specs=[row, row, row, row, pl.BlockSpec((1, 1, w), lambda i: (i, 0, 0))],
        compiler_params=_cparams("arbitrary"),
    )(proj, proj, proj, proj, cos, sin, ln_g.reshape(1, w), ln_b.reshape(1, w), ws, bias)


def _rope_tables(pos):
    half = MOBA_HEAD_DIM // 2
    inv = jnp.power(ROPE_THETA, -jnp.arange(half, dtype=F32) / half)
    ang = pos.astype(F32)[:, None] * inv[None, :]
    cos = jnp.cos(ang)
    sin = jnp.sin(ang)
    cos_full = jnp.tile(jnp.concatenate([cos, cos], axis=-1), (1, MOBA_HEADS))
    sin_signed = jnp.tile(jnp.concatenate([-sin, sin], axis=-1), (1, MOBA_HEADS))
    return cos_full, sin_signed


def _top_rows(gate, n_sel):
    idx = _iota(gate.shape, 0)
    sel = jnp.zeros(gate.shape, F32)
    cur = gate
    for _ in range(n_sel):
        m = jnp.max(cur, axis=0, keepdims=True)
        first = jnp.min(jnp.where(cur == m, idx, gate.shape[0]), axis=0, keepdims=True)
        one = jnp.logical_and(idx == first, m > NEG_INF)
        sel = jnp.where(one, 1.0, sel)
        cur = jnp.where(one, NEG_INF, cur)
    return sel


def _moba_prompt_kernel(q_ref, k_ref, vt_ref, km_ref, o_ref, sel_scr, *, group):
    qi = pl.program_id(2)
    n_heads = q_ref.shape[1]
    tq, d = q_ref.shape[2], q_ref.shape[3]
    nb = km_ref.shape[2]
    blk = k_ref.shape[3]

    qs, init = [], []
    for hh in range(n_heads):
        q = q_ref[0, hh]
        gate = _dot_nt(km_ref[0, hh], q)
        gate = jnp.where(_iota(gate.shape, 0) < qi, gate, NEG_INF)
        sel = _top_rows(gate, min(MOBA_TOPK, nb))
        for r in range(nb):
            sel_scr[hh, r] = sel[r:r + 1, :]
        q = (q.astype(F32) * (MOBA_HEAD_DIM ** -0.5)).astype(BF16)
        qs.append(q)
        s = _dot_nt(k_ref[0, hh, qi], q)
        s = jnp.where(_iota(s.shape, 0) <= _iota(s.shape, 1), s, NEG_INF)
        m0 = jnp.max(s, axis=0, keepdims=True)
        p = jnp.exp(s - m0)
        l0 = jnp.sum(p, axis=0, keepdims=True)
        init.append((m0, l0, _dot(vt_ref[0, hh, qi], p.astype(BF16))))

    def body(jg, carry):
        j0 = pl.multiple_of(jg * group, group)
        units = [(hh, g) for g in range(group) for hh in range(n_heads)]
        state = list(carry)
        score = lambda hh, g: _dot_nt(k_ref[0, hh, j0 + g], qs[hh])
        s_next = score(*units[0])
        for u, (hh, g) in enumerate(units):
            s = s_next
            if u + 1 < len(units):
                s_next = score(*units[u + 1])
            s = jnp.where(sel_scr[hh, j0 + g] > 0.5, s, NEG_INF)
            m_u = jnp.max(s, axis=0, keepdims=True)
            m_ref = jnp.where(m_u > NEG_INF, m_u, 0.0)
            p = jnp.exp(s - m_ref)
            l_u = jnp.sum(p, axis=0, keepdims=True)
            pv = _dot(vt_ref[0, hh, j0 + g], p.astype(BF16))
            m, l, acc = state[hh]
            m_new = jnp.maximum(m, m_u)
            w_old = jnp.exp(m - m_new)
            w_u = jnp.exp(m_u - m_new)
            state[hh] = (m_new, l * w_old + l_u * w_u, acc * w_old + pv * w_u)
        return tuple(state)

    final = lax.fori_loop(0, (qi + group - 1) // group, body, tuple(init))
    for hh in range(n_heads):
        _, l, acc = final[hh]
        o_ref[0, hh] = (acc / l).astype(BF16)


def _moba_prompt(q_hm, k_blk, vt_blk, km_hm):
    bsz, nh, seq, d = q_hm.shape
    nb = k_blk.shape[2]
    tq = MOBA_BLOCK
    group = math.gcd(nb, 4)
    hps = 2
    return pl.pallas_call(
        functools.partial(_moba_prompt_kernel, group=group),
        name="moba_prompt",
        out_shape=jax.ShapeDtypeStruct((bsz, nh, d, seq), BF16),
        grid=(bsz, nh // hps, seq // tq),
        in_specs=[pl.BlockSpec((1, hps, tq, d), lambda b, h, i: (b, h, i, 0)),
                  pl.BlockSpec((1, hps, nb, MOBA_BLOCK, d), lambda b, h, i: (b, h, 0, 0, 0)),
                  pl.BlockSpec((1, hps, nb, d, MOBA_BLOCK), lambda b, h, i: (b, h, 0, 0, 0)),
                  pl.BlockSpec((1, hps, nb, d), lambda b, h, i: (b, h, 0, 0))],
        out_specs=pl.BlockSpec((1, hps, d, tq), lambda b, h, i: (b, h, 0, i)),
        scratch_shapes=[pltpu.VMEM((hps, nb, 1, tq), F32)],
        compiler_params=_cparams("arbitrary", "arbitrary", "arbitrary"),
    )(q_hm, k_blk, vt_blk, km_hm)


def _bdot(a, b):
    return lax.dot_general(a, b, (((2,), (1,)), ((0,), (0,))), preferred_element_type=F32)


def _bdot_nt(a, b):
    return lax.dot_general(a, b, (((2,), (2,)), ((0,), (0,))), preferred_element_type=F32)


def _moba_sample_kernel(pt_ref, q_ref, kn_ref, vn_ref, k0_ref, k1_ref, v0_ref, v1_ref, o_ref,
                        o_scr, m_scr, l_scr, ks_scr, *, n_new):
    j = pl.program_id(1)
    nblk = pl.num_programs(1)
    q = q_ref[0]
    nh, rows, d = q.shape
    qs = (q.astype(F32) * (MOBA_HEAD_DIM ** -0.5)).astype(BF16)

    kt = [k0_ref[0], k1_ref[0]]
    vt = [v0_ref[0], v1_ref[0]]
    s = jnp.concatenate([_bdot(qs, k.astype(BF16)) for k in kt], axis=-1)
    m = jnp.max(s, axis=-1, keepdims=True)
    p = jnp.exp(s - m)
    l = jnp.sum(p, axis=-1, keepdims=True)
    pb = p.astype(BF16)
    o = (_bdot_nt(pb[:, :, :PAGE_SIZE], vt[0].astype(BF16))
         + _bdot_nt(pb[:, :, PAGE_SIZE:], vt[1].astype(BF16)))
    o_scr[j] = o
    m_scr[j] = jnp.broadcast_to(m, o.shape)
    l_scr[j] = jnp.broadcast_to(l, o.shape)
    ones = jnp.ones((nh, rows, PAGE_SIZE), BF16)
    ksum = None
    for k in kt:
        hi = k.astype(BF16)
        mid = (k - hi.astype(F32)).astype(BF16)
        part = _bdot_nt(ones, hi) + _bdot_nt(ones, mid)
        ksum = part if ksum is None else ksum + part
    ks_scr[j] = ksum

    @pl.when(j == nblk - 1)
    def _():
        nb = ks_scr.shape[0]
        kmean = (ks_scr[...] * (1.0 / MOBA_BLOCK)).astype(BF16).astype(F32)
        gate = jnp.sum(q.astype(F32)[None] * kmean, axis=-1, keepdims=True)
        sel = _top_rows(jnp.broadcast_to(gate, kmean.shape), min(MOBA_TOPK, nb)) > 0.5

        s_own = _bdot_nt(qs, kn_ref[0])
        causal = _iota(s_own.shape, 2) <= _iota(s_own.shape, 1)
        s_own = jnp.where(jnp.logical_and(causal, _iota(s_own.shape, 2) < n_new), s_own, NEG_INF)
        m_o = jnp.max(s_own, axis=-1, keepdims=True)
        p_o = jnp.exp(s_own - m_o)
        l_o = jnp.sum(p_o, axis=-1, keepdims=True)
        o_o = _bdot(p_o.astype(BF16), vn_ref[0])

        m_all = m_scr[...]
        m_top = jnp.maximum(m_o, jnp.max(jnp.where(sel, m_all, NEG_INF), axis=0))
        w = jnp.where(sel, jnp.exp(m_all - m_top[None]), 0.0)
        w_o = jnp.exp(m_o - m_top)
        l_tot = l_o * w_o + jnp.sum(w * l_scr[...], axis=0)
        o_tot = o_o * w_o + jnp.sum(w * o_scr[...], axis=0)
        o_ref[0] = o_tot / l_tot


def _moba_sample(page_table, q_hm, kn_hm, vn_hm, cache_kt, cache_vt, *, n_new):
    bsz, nh, rows, d = q_hm.shape
    n_pages = page_table.shape[1]
    pages_per_blk = MOBA_BLOCK // PAGE_SIZE
    assert pages_per_blk == 2 and n_pages % pages_per_blk == 0
    nblk = n_pages // pages_per_blk

    def page_spec(a):
        return pl.BlockSpec((1, nh, d, PAGE_SIZE), lambda b, j, pt: (pt[b * n_pages + 2 * j + a], 0, 0, 0))

    seq_spec = pl.BlockSpec((1, nh, rows, d), lambda b, j, pt: (b, 0, 0, 0))
    part = pltpu.VMEM((nblk, nh, rows, d), F32)
    grid_spec = pltpu.PrefetchScalarGridSpec(
        num_scalar_prefetch=1,
        grid=(bsz, nblk),
        in_specs=[seq_spec, seq_spec, seq_spec, page_spec(0), page_spec(1), page_spec(0), page_spec(1)],
        out_specs=seq_spec,
        scratch_shapes=[part, part, part, part])
    return pl.pallas_call(
        functools.partial(_moba_sample_kernel, n_new=n_new),
        name="moba_sample",
        out_shape=jax.ShapeDtypeStruct((bsz, nh, rows, d), F32),
        grid_spec=grid_spec,
        compiler_params=_cparams("arbitrary", "arbitrary"),
    )(page_table.reshape(-1), q_hm, kn_hm, vn_hm, cache_kt, cache_kt, cache_vt, cache_vt)


def _ssd_kernel(z_ref, xs_ref, bm_ref, cm_ref, dt_ref, conv0_ref, st0_ref, cw_ref, cb_ref, dtb_ref,
                alog_ref, dskip_ref, ng_ref, y_ref, stout_ref, convout_ref, st_scr, xbuf, yd_scr, *, lv):
    c = pl.program_id(1)
    last = pl.num_programs(1) - 1
    q = SSD_CHUNK
    hp = SSD_INNER
    gw = hp // SSD_GROUPS

    @pl.when(c == 0)
    def _():
        st_scr[...] = st0_ref[0]
        xbuf[0:8, :] = conv0_ref[0]

    def pad(x):
        if lv == q:
            return x
        return jnp.concatenate([x, jnp.zeros((q - lv, x.shape[1]), x.dtype)], axis=0)

    xbuf[8:8 + q, 0:hp] = pad(xs_ref[...])
    xbuf[8:8 + q, hp:hp + SSD_BC] = pad(bm_ref[...])
    xbuf[8:8 + q, hp + SSD_BC:] = pad(cm_ref[...])
    first = 8 - (SSD_CONV - 1)
    conv = cb_ref[...] + xbuf[first:first + q, :] * cw_ref[0:1, :]
    for tap in range(1, SSD_CONV):
        conv = conv + xbuf[first + tap:first + tap + q, :] * cw_ref[tap:tap + 1, :]

    @pl.when(c == last)
    def _():
        convout_ref[0] = xbuf[8 + lv - (SSD_CONV - 1):8 + lv, :]

    xbuf[0:8, :] = xbuf[q:q + 8, :]

    xbc = jax.nn.silu(conv)
    xs = xbc[:, :hp]
    bmat = xbc[:, hp:hp + SSD_BC]
    cmat = xbc[:, hp + SSD_BC:]

    dt = jax.nn.softplus(pad(dt_ref[...]) + dtb_ref[...])
    if lv != q:
        dt = jnp.where(_iota(dt.shape, 0) < lv, dt, 0.0)
    adt = dt * (-jnp.exp(alog_ref[...]))
    tri = (_iota((q, q), 1) <= _iota((q, q), 0)).astype(BF16)
    hi, mid, lo = _split3(adt)
    a_cum = _dot(tri, hi) + _dot(tri, mid) + _dot(tri, lo)
    a_cum_t = a_cum.T

    head_of_col = _iota((LANES, hp), 1) // SSD_HEAD_DIM
    e_cols = (head_of_col == _iota((LANES, hp), 0)).astype(BF16)
    e_time = (_iota((LANES, SSD_HEADS * q), 1) // q == _iota((LANES, SSD_HEADS * q), 0)).astype(BF16)
    dt_full = _dot_exact_rhs01(dt, e_cols)
    a_full = _dot_exact_rhs01(a_cum, e_cols)
    a_col = _dot_exact_rhs01(a_cum, e_time)
    a_last = a_full[q - 1:q, :]

    xdt = xs * dt_full
    xdt_b = xdt.astype(BF16)
    xdt_end = (xdt * jnp.exp(a_last - a_full)).astype(BF16)
    grow = jnp.exp(a_full)
    carry = jnp.exp(a_last)
    causal = _iota((q, q), 1) <= _iota((q, q), 0)
    low_half = _iota((q, LANES), 1) < SSD_HEAD_DIM
    heads_per_group = SSD_HEADS // SSD_GROUPS

    for g in range(SSD_GROUPS):
        cols = slice(g * gw, (g + 1) * gw)
        cc = cmat[:, g * SSD_STATE:(g + 1) * SSD_STATE].astype(BF16)
        bc = bmat[:, g * SSD_STATE:(g + 1) * SSD_STATE]
        cb = _dot_nt(cc, bc.astype(BF16))
        st_g = st_scr[:, cols]
        y_off = _dot(cc, st_g.astype(BF16)) * grow[:, cols]
        st_scr[:, cols] = st_g * carry[:, cols] + _dot(bc.T.astype(BF16), xdt_end[:, cols])
        for pair in range(heads_per_group // 2):
            h0 = g * heads_per_group + 2 * pair
            pcols = slice(h0 * SSD_HEAD_DIM, (h0 + 2) * SSD_HEAD_DIM)
            outs = []
            for h in (h0, h0 + 1):
                seg = a_col[:, h * q:(h + 1) * q] - a_cum_t[h:h + 1, :]
                decay = jnp.exp(jnp.where(causal, seg, NEG_INF))
                outs.append(_dot((cb * decay).astype(BF16), xdt_b[:, pcols]))
            yd_scr[:, pcols] = jnp.where(low_half, outs[0], outs[1]) + y_off[:, pcols.start - g * gw:pcols.stop - g * gw]

    y = yd_scr[...] + dskip_ref[...] * xs
    y = y * jax.nn.silu(pad(z_ref[...]))
    y = y * lax.rsqrt(jnp.mean(y * y, axis=-1, keepdims=True) + EPS) * ng_ref[...]
    y_ref[...] = y[:lv].astype(y_ref.dtype)

    @pl.when(c == last)
    def _():
        stout_ref[0] = st_scr[...]


def _ssd_scan(proj, conv0p, st0_t, conv_w, conv_b, dt_bias, a_log, dskip_full, norm_g, *, bsz, seq_len):
    lv = min(seq_len, SSD_CHUNK)
    nc = seq_len // lv
    hp = SSD_INNER
    t = proj.shape[0]
    blk = lambda w, cidx: pl.BlockSpec((lv, w), lambda b, c: (b * nc + c, cidx))
    full = lambda shape: pl.BlockSpec(shape, lambda b, c: (0,) * len(shape))
    per_seq = lambda shape: pl.BlockSpec((1,) + shape, lambda b, c: (b,) + (0,) * len(shape))
    return pl.pallas_call(
        functools.partial(_ssd_kernel, lv=lv),
        name="ssd_scan",
        out_shape=[jax.ShapeDtypeStruct((t, hp), BF16 if lv % 16 == 0 else F32),
                   jax.ShapeDtypeStruct((bsz, SSD_STATE, hp), F32),
                   jax.ShapeDtypeStruct((bsz, SSD_CONV - 1, SSD_CONV_DIM), F32)],
        grid=(bsz, nc),
        in_specs=[blk(hp, 0), blk(hp, 1), blk(SSD_BC, 2 * hp // SSD_BC), blk(SSD_BC, 2 * hp // SSD_BC + 1),
                  blk(LANES, (2 * hp + 2 * SSD_BC) // LANES),
                  per_seq((8, SSD_CONV_DIM)), per_seq((SSD_STATE, hp)),
                  full((SSD_CONV, SSD_CONV_DIM)), full((1, SSD_CONV_DIM)), full((1, LANES)), full((1, LANES)),
                  full((1, hp)), full((1, hp))],
        out_specs=[pl.BlockSpec((lv, hp), lambda b, c: (b * nc + c, 0)),
                   per_seq((SSD_STATE, hp)), per_seq((SSD_CONV - 1, SSD_CONV_DIM))],
        scratch_shapes=[pltpu.VMEM((SSD_STATE, hp), F32), pltpu.VMEM((SSD_CHUNK + 8, SSD_CONV_DIM), F32),
                        pltpu.VMEM((SSD_CHUNK, hp), F32)],
        compiler_params=_cparams("arbitrary", "arbitrary"),
    )(proj, proj, proj, proj, proj, conv0p, st0_t, conv_w, conv_b.reshape(1, -1), dt_bias, a_log,
      dskip_full, norm_g.reshape(1, -1))


def _argmax_rows(x):
    rows, n = x.shape
    idx = _iota((rows, n), 0)
    vs = [x[r:r + 8] for r in range(0, rows, 8)]
    ids = [idx[r:r + 8] for r in range(0, rows, 8)]
    while len(vs) > 1:
        nv, ni = [], []
        for p in range(0, len(vs) - 1, 2):
            nv.append(jnp.maximum(vs[p], vs[p + 1]))
            ni.append(jnp.where(vs[p] >= vs[p + 1], ids[p], ids[p + 1]))
        if len(vs) % 2:
            nv.append(vs[-1])
            ni.append(ids[-1])
        vs, ids = nv, ni
    m = jnp.max(vs[0], axis=0, keepdims=True)
    first = jnp.min(jnp.where(vs[0] == m, ids[0], rows), axis=0, keepdims=True)
    return m, first


def _extract_top(cur_scr, n_chain, k):
    rows, n = cur_scr.shape[1], cur_scr.shape[2]
    idx = _iota((rows, n), 0)
    slot = _iota((k, n), 0)

    def body(i, carry):
        out = []
        for c in range(n_chain):
            vals, picks = carry[c]
            cur = cur_scr[c]
            m, first = _argmax_rows(cur)
            cur_scr[c] = jnp.where(idx == first, NEG_INF, cur)
            out.append((jnp.where(slot == i, m, vals), jnp.where(slot == i, first, picks)))
        return tuple(out)

    init = tuple((jnp.zeros((k, n), F32), jnp.zeros((k, n), jnp.int32)) for _ in range(n_chain))
    return lax.fori_loop(0, k, body, init)


def _candidate_pieces(k):
    assert k == 16
    pieces = [(0, 1, 16, 16), (1, 2, 8, 8)]
    pieces += [(a, a + 1, 8, k // (a + 1)) for a in range(2, 8)]
    pieces.append((8, 16, 1, 1))
    return pieces


def _peer_route_kernel(q_ref, sk_ref, alpha_ref, count_ref, beta_ref, rank_ref, cur_scr, cand_scr):
    k = PEER_TOPK
    hd = sk_ref.shape[3]
    nk = sk_ref.shape[2]
    pieces = _candidate_pieces(k)
    n_lt = q_ref.shape[0] // LANES
    idx = _iota((nk, LANES), 0)

    scores = []
    for lt in range(n_lt):
        q = q_ref[lt * LANES:(lt + 1) * LANES, :]
        for c in range(2):
            s = _dot_nt(sk_ref[0, c], q[:, c * hd:(c + 1) * hd])
            cur_scr[2 * lt + c] = s
            scores.append(s)
    tops = _extract_top(cur_scr, 2 * n_lt, k)

    for lt in range(n_lt):
        v1, v2 = tops[2 * lt][0], tops[2 * lt + 1][0]
        cands = []
        for a_lo, a_hi, n_b, n_valid in pieces:
            if a_hi - a_lo == 1:
                c = v1[a_lo:a_hi, :] + v2[0:n_b, :]
                cands.append(jnp.where(_iota(c.shape, 0) < n_valid, c, NEG_INF))
            else:
                cands.append(v1[a_lo:a_hi, :] + v2[0:1, :])
        cand_scr[lt] = jnp.concatenate(cands, axis=0)
    picked = _extract_top(cand_scr, n_lt, k)

    for lt in range(n_lt):
        lanes = slice(lt * LANES, (lt + 1) * LANES)
        (v1, rows1), (v2, rows2) = tops[2 * lt], tops[2 * lt + 1]
        best, best_rows = picked[lt]
        z = jnp.sum(jnp.exp(best - best[0:1, :]), axis=0, keepdims=True)
        count = jnp.zeros((nk, LANES), F32)
        rank2 = jnp.full((nk, LANES), float(k), F32)
        row = 0
        for a_lo, a_hi, n_b, _ in pieces:
            for a in range(a_lo, a_hi):
                in_piece = jnp.logical_and(best_rows >= row, best_rows < row + n_b)
                n_a = jnp.sum(jnp.where(in_piece, 1.0, 0.0), axis=0, keepdims=True)
                count = jnp.where(idx == rows1[a:a + 1, :], n_a, count)
                row += n_b
        for b in range(k):
            rank2 = jnp.where(idx == rows2[b:b + 1, :], float(b), rank2)
        alpha_ref[0, :, lanes] = jnp.exp(scores[2 * lt] - v1[0:1, :]) / z
        count_ref[0, :, lanes] = count
        beta_ref[0, :, lanes] = jnp.exp(scores[2 * lt + 1] - v2[0:1, :])
        rank_ref[0, :, lanes] = rank2


def _peer_route(q, subkeys, *, tt):
    t = q.shape[0]
    nh, _, nk, hd = subkeys.shape
    out = jax.ShapeDtypeStruct((nh, nk, t), F32)
    ospec = pl.BlockSpec((1, nk, tt), lambda i, h: (h, 0, i))
    n_cand = sum((a_hi - a_lo) * n_b for a_lo, a_hi, n_b, _ in _candidate_pieces(PEER_TOPK))
    return pl.pallas_call(
        _peer_route_kernel,
        name="peer_route",
        out_shape=[out, out, out, out],
        grid=(t // tt, nh),
        in_specs=[pl.BlockSpec((tt, 2 * hd), lambda i, h: (i, h)),
                  pl.BlockSpec((1, 2, nk, hd), lambda i, h: (h, 0, 0, 0))],
        out_specs=[ospec, ospec, ospec, ospec],
        scratch_shapes=[pltpu.VMEM((2 * (tt // LANES), nk, LANES), F32),
                        pltpu.VMEM((tt // LANES, n_cand, LANES), F32)],
        compiler_params=_cparams("arbitrary", "arbitrary"),
    )(q, subkeys)


def _peer_dense_kernel(h_ref, u_ref, vt_ref, alpha_ref, count_ref, beta_ref, rank_ref, o_ref, aw_scr, *, ib):
    e = pl.program_id(1)
    nh, nk, tt = beta_ref.shape

    @pl.when(e == 0)
    def _():
        o_ref[...] = jnp.zeros(o_ref.shape, F32)

    i0 = pl.multiple_of(e * ib, ib)
    hx = h_ref[...]
    per_sub = 2
    sub = per_sub * nk
    n_sub = ib // per_sub

    s_next = _dot_nt(u_ref[0:sub, :], hx)
    acc = None
    for c in range(n_sub):
        s_cur = s_next
        if c + 1 < n_sub:
            s_next = _dot_nt(u_ref[(c + 1) * sub:(c + 2) * sub, :], hx)
        for lt in range(tt // LANES):
            lanes = slice(lt * LANES, (lt + 1) * LANES)
            counts = [count_ref[h, pl.ds(i0, ib), lanes] for h in range(nh)]
            alphas = [alpha_ref[h, pl.ds(i0, ib), lanes] for h in range(nh)]
            for kk in range(per_sub):
                ii = c * per_sub + kk
                w = jnp.zeros((nk, LANES), F32)
                for h in range(nh):
                    hit = rank_ref[h, :, lanes] < counts[h][ii:ii + 1, :]
                    w = w + jnp.where(hit, beta_ref[h, :, lanes], 0.0) * alphas[h][ii:ii + 1, :]
                aw_scr[ii * nk:(ii + 1) * nk, lanes] = (w * _gelu(s_cur[kk * nk:(kk + 1) * nk, lanes])).astype(BF16)
        part = _dot(vt_ref[:, c * sub:(c + 1) * sub], aw_scr[c * sub:(c + 1) * sub, :])
        acc = part if acc is None else acc + part
    o_ref[...] += acc


def _peer_dense(h, u_tab, vt_tab, route, *, tt, ib):
    t, d = h.shape
    alpha, count, beta, rank = route
    nh, nk, _ = alpha.shape
    te = ib * nk
    rspec = pl.BlockSpec((nh, nk, tt), lambda i, e: (0, 0, i))
    return pl.pallas_call(
        functools.partial(_peer_dense_kernel, ib=ib),
        name="peer_dense",
        out_shape=jax.ShapeDtypeStruct((d, t), F32),
        grid=(t // tt, nk // ib),
        in_specs=[pl.BlockSpec((tt, d), lambda i, e: (i, 0)),
                  pl.BlockSpec((te, d), lambda i, e: (e, 0)),
                  pl.BlockSpec((d, te), lambda i, e: (0, e)),
                  rspec, rspec, rspec, rspec],
        out_specs=pl.BlockSpec((d, tt), lambda i, e: (0, i)),
        scratch_shapes=[pltpu.VMEM((te, tt), BF16)],
        compiler_params=_cparams("arbitrary", "arbitrary"),
    )(h, u_tab, vt_tab, alpha, count, beta, rank)


def _prep_weights(norm_mix_g, mod_mix_w, mod_mix_b, norm_ffn_g, mod_ffn_w, mod_ffn_b,
                  w_in_ab, w_out_ab, gmlp_ln_g, gmlp_ln_b, gmlp_ws, gmlp_bs,
                  ssd_w_in, ssd_conv_w, ssd_conv_b, ssd_dt_bias, ssd_a_log, ssd_d, ssd_norm_g, ssd_w_out,
                  peer_wq, peer_subkeys, peer_u, peer_v, final_norm_g):
    depth = norm_mix_g.shape[0]
    d = norm_mix_g.shape[1]
    p = dict(depth=depth, norm_mix_g=norm_mix_g, norm_ffn_g=norm_ffn_g, final_norm_g=final_norm_g,
             gmlp_ln_g=gmlp_ln_g, gmlp_ln_b=gmlp_ln_b, gmlp_ws=gmlp_ws, gmlp_bs=gmlp_bs,
             ssd_conv_w=ssd_conv_w, ssd_conv_b=ssd_conv_b, ssd_norm_g=ssd_norm_g)
    mod_w, mod_b = [], []
    for i in range(depth):
        mod_w += [mod_mix_w[i], mod_ffn_w[i]]
        mod_b += [mod_mix_b[i], mod_ffn_b[i]]
    p["mod_w"] = jnp.stack(mod_w).astype(BF16)
    p["mod_b"] = jnp.stack(mod_b)[:, None, :]
    p["w_in_ab"] = w_in_ab.astype(BF16)
    p["w_out_ab"] = w_out_ab.astype(BF16)
    n_ssd = ssd_w_in.shape[0]
    z_w = ssd_w_in[:, :, :SSD_INNER]
    xbc_w = ssd_w_in[:, :, SSD_INNER:SSD_INNER + SSD_CONV_DIM]
    dt_w = ssd_w_in[:, :, SSD_INNER + SSD_CONV_DIM:]
    used = 2 * SSD_INNER + 2 * SSD_BC + LANES
    total = -(-used // 1792) * 1792
    p["ssd_w_in"] = jnp.concatenate(
        [z_w, xbc_w, dt_w, jnp.zeros((n_ssd, d, total - used + LANES - SSD_HEADS), F32)], axis=-1).astype(BF16)
    pad_heads = lambda a, v: jnp.concatenate([a, jnp.full((n_ssd, LANES - SSD_HEADS), v, F32)], axis=-1)[:, None, :]
    p["ssd_dt_bias"] = pad_heads(ssd_dt_bias, 0.0)
    p["ssd_a_log"] = pad_heads(ssd_a_log, 0.0)
    p["ssd_dskip"] = jnp.repeat(ssd_d, SSD_HEAD_DIM, axis=-1)[:, None, :]
    p["ssd_w_out"] = ssd_w_out.astype(BF16)
    p["peer_wq"] = peer_wq.astype(BF16)
    p["peer_subkeys"] = peer_subkeys.astype(BF16)
    p["peer_u"] = peer_u.astype(BF16)
    p["peer_vt"] = jnp.swapaxes(peer_v, 1, 2).astype(BF16)
    return p


def _peer_layer(x, mod, seq_len, p, i, *, tm, final):
    q, h = _adaln_matmul(x, p["norm_ffn_g"][i], mod, seq_len, p["peer_wq"][i], tm=tm,
                         tn=p["peer_wq"].shape[2] // 2, out_dtype=BF16, emit_h=True)
    nk = p["peer_subkeys"].shape[3]
    route = _peer_route(q, p["peer_subkeys"][i], tt=min(256, tm))
    y_t = _peer_dense(h, p["peer_u"][i], p["peer_vt"][i], route, tt=min(256, tm), ib=8)
    return _residual_t(x, y_t, mod, seq_len, tm=tm, final_g=p["final_norm_g"] if final else None)


def _ab_layer(x, mod, bsz, seq_len, pos, past, p, i, j, *, tm):
    t, d = x.shape
    w = GMLP_GROUPS * (d // 16)
    proj = _adaln_matmul(x, p["norm_mix_g"][i], mod, seq_len, p["w_in_ab"][j], tm=tm,
                         tn=p["w_in_ab"].shape[2] // 2)
    cos, sin = _rope_tables(pos)
    lc = min(seq_len, GMLP_CHUNK)
    ws = p["gmlp_ws"][j][:, :lc, :lc]
    bias = jnp.repeat(p["gmlp_bs"][j][:, :lc].T, w // GMLP_GROUPS, axis=1)
    if seq_len < GMLP_CHUNK:
        reps = tm // seq_len
        ws = jnp.where(jnp.tril(jnp.ones((lc, lc), bool)), ws, 0.0)
        eye = jnp.eye(reps, dtype=F32)
        ws = jnp.einsum("ab,gts->gatbs", eye, ws).reshape(GMLP_GROUPS, tm, tm)
        bias = jnp.tile(bias, (reps, 1))
        cos = jnp.tile(cos, (reps, 1))
        sin = jnp.tile(sin, (reps, 1))
        tr, lc_eff = tm, tm
    else:
        tr, lc_eff = MOBA_BLOCK, lc
    a_out, vln, q_rot, k_rot, kmean = _ab_post(proj, cos, sin, p["gmlp_ln_g"][j], p["gmlp_ln_b"][j],
                                               ws.astype(BF16), bias, tr=tr, lc=lc_eff)
    v_new = proj[:, 2 * w + 2 * w:2 * w + 3 * w]
    nh, hd = MOBA_HEADS, MOBA_HEAD_DIM
    if past is None:
        nb = seq_len // MOBA_BLOCK
        q_hm = q_rot.reshape(bsz, seq_len, nh, hd).transpose(0, 2, 1, 3)
        k_blk = k_rot.astype(BF16).reshape(bsz, nb, MOBA_BLOCK, nh, hd).transpose(0, 3, 1, 2, 4)
        vt_blk = v_new.astype(BF16).reshape(bsz, nb, MOBA_BLOCK, nh, hd).transpose(0, 3, 1, 4, 2)
        km_hm = kmean.reshape(bsz, nb, nh, hd).transpose(0, 2, 1, 3).astype(BF16)
        o_t = _moba_prompt(q_hm, k_blk, vt_blk, km_hm)
        b_out = o_t.transpose(0, 3, 1, 2).reshape(t, w)
    else:
        page_table, cache_k, cache_v = past
        rows = -(-seq_len // 16) * 16

        def head_major(a):
            a = a.astype(BF16).reshape(bsz, seq_len, nh, hd).transpose(0, 2, 1, 3)
            return jnp.pad(a, ((0, 0), (0, 0), (0, rows - seq_len), (0, 0)))

        b_out = _moba_sample(page_table, head_major(q_rot), head_major(k_rot), head_major(v_new),
                             cache_k.transpose(0, 2, 3, 1), cache_v.transpose(0, 2, 3, 1), n_new=seq_len)
        b_out = b_out[:, :, :seq_len].transpose(0, 2, 1, 3).reshape(t, w).astype(BF16)
    x = _proj_residual([a_out, b_out], p["w_out_ab"][j], x, mod, seq_len, tm=tm)
    lcv = min(seq_len, GMLP_CHUNK)
    gv = vln.reshape(bsz, seq_len, w)[:, seq_len - lcv:]
    return (x, k_rot.reshape(bsz, seq_len, nh, hd), v_new.reshape(bsz, seq_len, nh, hd), gv)


def _ssd_layer(x, mod, bsz, seq_len, ssm0, conv0, p, i, j, *, tm):
    t, d = x.shape
    proj = _adaln_matmul(x, p["norm_mix_g"][i], mod, seq_len, p["ssd_w_in"][j], tm=tm, tn=1792)
    if conv0 is None:
        conv0 = jnp.zeros((bsz, SSD_CONV - 1, SSD_CONV_DIM), F32)
    if ssm0 is None:
        st0_t = jnp.zeros((bsz, SSD_STATE, SSD_INNER), F32)
    else:
        st0_t = ssm0.reshape(bsz, SSD_INNER, SSD_STATE).transpose(0, 2, 1)
    conv0p = jnp.pad(conv0, ((0, 0), (8 - (SSD_CONV - 1), 0), (0, 0)))
    y, st_t, conv_new = _ssd_scan(proj, conv0p, st0_t, p["ssd_conv_w"][j], p["ssd_conv_b"][j],
                                  p["ssd_dt_bias"][j], p["ssd_a_log"][j], p["ssd_dskip"][j],
                                  p["ssd_norm_g"][j], bsz=bsz, seq_len=seq_len)
    x = _proj_residual([y], p["ssd_w_out"][j], x, mod, seq_len, tm=tm)
    ssm_new = st_t.transpose(0, 2, 1).reshape(bsz, SSD_HEADS, SSD_HEAD_DIM, SSD_STATE)
    return x, ssm_new, conv_new


def _run_trunk(x, c, pos0, past_kv, ssm_past, conv_past, p):
    bsz, seq_len, d = x.shape
    t = bsz * seq_len
    tm = 512 if t % 512 == 0 and seq_len % 512 == 0 else 256
    assert t % tm == 0
    x = x.reshape(t, d)
    pos = pos0 + jnp.arange(seq_len, dtype=jnp.int32)
    mods = _mod_all(c, p["mod_w"], p["mod_b"])
    depth = p["depth"]
    k_rows, v_rows, gv_rows, ssm_new, conv_new = [], [], [], [], []
    for i in range(depth):
        j = i // 2
        if i % 2 == 0:
            past = None if past_kv is None else (past_kv[0], past_kv[1][j], past_kv[2][j])
            x, k_new, v_new, gv = _ab_layer(x, mods[2 * i], bsz, seq_len, pos, past, p, i, j, tm=tm)
            k_rows.append(k_new)
            v_rows.append(v_new)
            gv_rows.append(gv)
        else:
            x, s_new, cv_new = _ssd_layer(x, mods[2 * i], bsz, seq_len,
                                          None if ssm_past is None else ssm_past[j],
                                          None if conv_past is None else conv_past[j], p, i, j, tm=tm)
            ssm_new.append(s_new)
            conv_new.append(cv_new)
        x = _peer_layer(x, mods[2 * i + 1], seq_len, p, i, tm=tm, final=(i == depth - 1))
    y = x.reshape(bsz, seq_len, d)
    return y, jnp.stack(k_rows), jnp.stack(v_rows), jnp.stack(gv_rows), jnp.stack(ssm_new), jnp.stack(conv_new)


def kernel(x_prompt, x_sample, cache_k, cache_v, page_table, state_ssm, state_conv, c_prompt, c_sample,
           norm_mix_g, mod_mix_w, mod_mix_b, norm_ffn_g, mod_ffn_w, mod_ffn_b,
           w_in_ab, w_out_ab, gmlp_ln_g, gmlp_ln_b, gmlp_ws, gmlp_bs,
           ssd_w_in, ssd_conv_w, ssd_conv_b, ssd_dt_bias, ssd_a_log, ssd_d, ssd_norm_g, ssd_w_out,
           peer_wq, peer_subkeys, peer_u, peer_v, final_norm_g):
    p = _prep_weights(norm_mix_g, mod_mix_w, mod_mix_b, norm_ffn_g, mod_ffn_w, mod_ffn_b,
                      w_in_ab, w_out_ab, gmlp_ln_g, gmlp_ln_b, gmlp_ws, gmlp_bs,
                      ssd_w_in, ssd_conv_w, ssd_conv_b, ssd_dt_bias, ssd_a_log, ssd_d, ssd_norm_g, ssd_w_out,
                      peer_wq, peer_subkeys, peer_u, peer_v, final_norm_g)
    prompt = _run_trunk(x_prompt, c_prompt, 0, None, None, None, p)
    past_len = page_table.shape[1] * PAGE_SIZE
    sample = _run_trunk(x_sample, c_sample, past_len, (page_table, cache_k, cache_v),
                        state_ssm, state_conv, p)
    return (prompt[0], sample[0]) + prompt[1:] + sample[1:]
```

```python
import functools
import math

import jax
import jax.numpy as jnp
from jax import lax
from jax.experimental import pallas as pl
from jax.experimental.pallas import tpu as pltpu

F32 = jnp.float32
BF16 = jnp.bfloat16
NEG_INF = float("-inf")

EPS = 1e-6
LANES = 128
GMLP_GROUPS = 8
GMLP_CHUNK = 128
MOBA_HEADS = 8
MOBA_HEAD_DIM = 64
MOBA_BLOCK = 256
MOBA_TOPK = 3
PAGE_SIZE = 128
ROPE_THETA = 10000.0
SSD_HEADS = 32
SSD_HEAD_DIM = 64
SSD_GROUPS = 4
SSD_STATE = 128
SSD_CONV = 4
SSD_CHUNK = 128
SSD_INNER = SSD_HEADS * SSD_HEAD_DIM
SSD_BC = SSD_GROUPS * SSD_STATE
SSD_CONV_DIM = SSD_INNER + 2 * SSD_BC
PEER_HEADS = 8
PEER_TOPK = 16
PEER_FIRST_KEYS_PER_SUBTILE = 2
PEER_SUBTILES_PER_STEP = 4
VMEM_LIMIT_BYTES = 56 * 1024 * 1024


def _cparams(*sem):
    return pltpu.CompilerParams(dimension_semantics=sem, vmem_limit_bytes=VMEM_LIMIT_BYTES)


def _dot(a, b):
    return jnp.dot(a, b, preferred_element_type=F32)


def _dot_nt(a, b):
    return lax.dot_general(a, b, (((1,), (1,)), ((), ())), preferred_element_type=F32)


def _iota(shape, dim):
    return lax.broadcasted_iota(jnp.int32, shape, dim)


def _gelu(x):
    return 0.5 * x * (1.0 + lax.erf(x * (2.0 ** -0.5)))


def _split3(x):
    hi = x.astype(BF16)
    r = x - hi.astype(F32)
    mid = r.astype(BF16)
    lo = (r - mid.astype(F32)).astype(BF16)
    return hi, mid, lo


def _dot_exact_rhs01(x, e):
    hi, mid, lo = _split3(x)
    return _dot(hi, e) + _dot(mid, e) + _dot(lo, e)


def _mod_kernel(c_ref, w_ref, b_ref, o_ref):
    o_ref[0] = _dot(c_ref[...].astype(BF16), w_ref[0]) + b_ref[0]


def _mod_all(c, w4, b4):
    bsz, d = c.shape
    rows = -(-bsz // 16) * 16
    cp = jnp.pad(c, ((0, rows - bsz), (0, 0)))
    n = w4.shape[0]
    out = pl.pallas_call(
        _mod_kernel,
        name="adaln_mod",
        out_shape=jax.ShapeDtypeStruct((n, rows, 3 * d), F32),
        grid=(n, 3),
        in_specs=[pl.BlockSpec((rows, d), lambda i, j: (0, 0)),
                  pl.BlockSpec((1, d, d), lambda i, j: (i, 0, j)),
                  pl.BlockSpec((1, 1, d), lambda i, j: (i, 0, j))],
        out_specs=pl.BlockSpec((1, rows, d), lambda i, j: (i, 0, j)),
        compiler_params=_cparams("arbitrary", "arbitrary"),
    )(cp, w4, b4)
    return out[:, :bsz]


def _mod_rows(mod, seq_len, tm):
    bsz, w = mod.shape
    if seq_len % tm == 0:
        return mod[:, None, :], seq_len // tm
    assert tm % seq_len == 0 and (bsz * seq_len) % tm == 0
    return jnp.repeat(mod, seq_len, axis=0).reshape(bsz * seq_len // tm, tm, w), 1


def _adaln_mm_kernel(x_ref, g_ref, sh_ref, sc_ref, w_ref, o_ref, *rest, emit_h):
    if emit_h:
        h_ref, h_scr = rest
    else:
        (h_scr,) = rest

    @pl.when(pl.program_id(1) == 0)
    def _():
        x = x_ref[...]
        y = x * lax.rsqrt(jnp.mean(x * x, axis=-1, keepdims=True) + EPS)
        y = y * g_ref[...]
        h = (y * (1.0 + sc_ref[0]) + sh_ref[0]).astype(BF16)
        h_scr[...] = h
        if emit_h:
            h_ref[...] = h

    o_ref[...] = _dot(h_scr[...], w_ref[...]).astype(o_ref.dtype)


def _adaln_matmul(x, g, mod, seq_len, w, *, tm, tn, out_dtype=F32, emit_h=False):
    t, d = x.shape
    n = w.shape[1]
    assert t % tm == 0 and n % tn == 0
    mod3, tpg = _mod_rows(mod, seq_len, tm)
    r = mod3.shape[1]
    out_shape = [jax.ShapeDtypeStruct((t, n), out_dtype)]
    out_specs = [pl.BlockSpec((tm, tn), lambda i, j: (i, j))]
    if emit_h:
        out_shape.append(jax.ShapeDtypeStruct((t, d), BF16))
        out_specs.append(pl.BlockSpec((tm, d), lambda i, j: (i, 0)))
    res = pl.pallas_call(
        functools.partial(_adaln_mm_kernel, emit_h=emit_h),
        name="adaln_matmul",
        out_shape=out_shape,
        grid=(t // tm, n // tn),
        in_specs=[pl.BlockSpec((tm, d), lambda i, j: (i, 0)),
                  pl.BlockSpec((1, d), lambda i, j: (0, 0)),
                  pl.BlockSpec((1, r, d), lambda i, j: (i // tpg, 0, 0)),
                  pl.BlockSpec((1, r, d), lambda i, j: (i // tpg, 0, 1)),
                  pl.BlockSpec((d, tn), lambda i, j: (0, j))],
        out_specs=out_specs,
        scratch_shapes=[pltpu.VMEM((tm, d), BF16)],
        compiler_params=_cparams("arbitrary", "arbitrary"),
    )(x, g.reshape(1, d), mod3, mod3, w)
    return res if emit_h else res[0]


def _proj_res_kernel(*refs, ks):
    lhs = refs[:len(ks)]
    w_ref, x_ref, gate_ref, o_ref = refs[len(ks):]
    acc = None
    off = 0
    for l_ref, k in zip(lhs, ks):
        part = _dot(l_ref[...].astype(BF16), w_ref[off:off + k, :])
        acc = part if acc is None else acc + part
        off += k
    o_ref[...] = x_ref[...] + gate_ref[0] * acc


def _proj_residual(lhs, w, x, mod, seq_len, *, tm):
    t, d = x.shape
    ks = tuple(l.shape[1] for l in lhs)
    mod3, tpg = _mod_rows(mod, seq_len, tm)
    r = mod3.shape[1]
    in_specs = [pl.BlockSpec((tm, k), lambda i: (i, 0)) for k in ks]
    in_specs += [pl.BlockSpec((sum(ks), d), lambda i: (0, 0)),
                 pl.BlockSpec((tm, d), lambda i: (i, 0)),
                 pl.BlockSpec((1, r, d), lambda i: (i // tpg, 0, 2))]
    return pl.pallas_call(
        functools.partial(_proj_res_kernel, ks=ks),
        name="proj_residual",
        out_shape=jax.ShapeDtypeStruct((t, d), F32),
        grid=(t // tm,),
        in_specs=in_specs,
        out_specs=pl.BlockSpec((tm, d), lambda i: (i, 0)),
        compiler_params=_cparams("arbitrary"),
    )(*lhs, w, x, mod3)


def _res_kernel(x_ref, yt_ref, gate_ref, *rest, final_norm):
    x2 = x_ref[...] + gate_ref[0] * yt_ref[...].T
    if final_norm:
        g_ref, o_ref = rest
        o_ref[...] = x2 * lax.rsqrt(jnp.mean(x2 * x2, axis=-1, keepdims=True) + EPS) * g_ref[...]
    else:
        (o_ref,) = rest
        o_ref[...] = x2


def _residual_t(x, y_t, mod, seq_len, *, tm, final_g=None):
    t, d = x.shape
    mod3, tpg = _mod_rows(mod, seq_len, tm)
    r = mod3.shape[1]
    in_specs = [pl.BlockSpec((tm, d), lambda i: (i, 0)),
                pl.BlockSpec((d, tm), lambda i: (0, i)),
                pl.BlockSpec((1, r, d), lambda i: (i // tpg, 0, 2))]
    args = [x, y_t, mod3]
    if final_g is not None:
        in_specs.append(pl.BlockSpec((1, d), lambda i: (0, 0)))
        args.append(final_g.reshape(1, d))
    return pl.pallas_call(
        functools.partial(_res_kernel, final_norm=final_g is not None),
        name="peer_residual",
        out_shape=jax.ShapeDtypeStruct((t, d), F32),
        grid=(t // tm,),
        in_specs=in_specs,
        out_specs=pl.BlockSpec((tm, d), lambda i: (i, 0)),
        compiler_params=_cparams("arbitrary"),
    )(*args)


def _rope(x, cos, sin_signed):
    lane = _iota(x.shape, 1) % MOBA_HEAD_DIM
    half = MOBA_HEAD_DIM // 2
    width = x.shape[1]
    partner = jnp.where(lane < half, pltpu.roll(x, width - half, 1), pltpu.roll(x, half, 1))
    return x * cos + partner * sin_signed


def _ab_post_kernel(u_ref, v_ref, q_ref, k_ref, cos_ref, sin_ref, lng_ref, lnb_ref, ws_ref, bias_ref,
                    a_ref, vln_ref, qr_ref, kr_ref, km_ref, *, lc, n_chunks):
    u = _gelu(u_ref[...])
    v = _gelu(v_ref[...])
    mu = jnp.mean(v, axis=-1, keepdims=True)
    var = jnp.mean(jnp.square(v - mu), axis=-1, keepdims=True)
    vln = (v - mu) * lax.rsqrt(var + EPS) * lng_ref[...] + lnb_ref[...]
    vln_ref[...] = vln

    width = vln.shape[1]
    lane_group = _iota((lc, width), 1) // (width // GMLP_GROUPS)
    causal = _iota((lc, lc), 1) <= _iota((lc, lc), 0)
    w_tril = [jnp.where(causal, ws_ref[g], jnp.zeros((), BF16)) for g in range(GMLP_GROUPS)]
    for c in range(n_chunks):
        rows = slice(c * lc, (c + 1) * lc)
        vc = vln[rows].astype(BF16)
        mix = jnp.zeros((lc, width), F32)
        for g in range(GMLP_GROUPS):
            mix = mix + jnp.where(lane_group == g, _dot(w_tril[g], vc), 0.0)
        a_ref[rows, :] = (u[rows] * (mix + bias_ref[...])).astype(BF16)

    cos = cos_ref[...]
    sin = sin_ref[...]
    qr_ref[...] = _rope(q_ref[...], cos, sin).astype(BF16)
    kr = _rope(k_ref[...], cos, sin)
    kr_ref[...] = kr
    km_ref[0] = jnp.mean(kr, axis=0, keepdims=True)


def _ab_post(proj, cos, sin, ln_g, ln_b, ws, bias, *, tr, lc):
    t = proj.shape[0]
    w = ln_g.shape[0]
    n_tab = cos.shape[0] // tr
    n_tiles = t // tr
    col = lambda c: pl.BlockSpec((tr, w), lambda i: (i, c))
    tab = pl.BlockSpec((tr, w), lambda i: (i % n_tab, 0))
    row = pl.BlockSpec((tr, w), lambda i: (i, 0))
    return pl.pallas_call(
        functools.partial(_ab_post_kernel, lc=lc, n_chunks=tr // lc),
        name="gmlp_rope",
        out_shape=[jax.ShapeDtypeStruct((t, w), BF16),
                   jax.ShapeDtypeStruct((t, w), F32),
                   jax.ShapeDtypeStruct((t, w), BF16),
                   jax.ShapeDtypeStruct((t, w), F32),
                   jax.ShapeDtypeStruct((n_tiles, 1, w), F32)],
        grid=(n_tiles,),
        in_specs=[col(0), col(1), col(2), col(3), tab, tab,
                  pl.BlockSpec((1, w), lambda i: (0, 0)),
                  pl.BlockSpec((1, w), lambda i: (0, 0)),
                  pl.BlockSpec((GMLP_GROUPS, lc, lc), lambda i: (0, 0, 0)),
                  pl.BlockSpec((lc, w), lambda i: (0, 0))],
        out_specs=[row, row, row, row, pl.BlockSpec((1, 1, w), lambda i: (i, 0, 0))],
        compiler_params=_cparams("arbitrary"),
    )(proj, proj, proj, proj, cos, sin, ln_g.reshape(1, w), ln_b.reshape(1, w), ws, bias)


def _rope_tables(pos):
    half = MOBA_HEAD_DIM // 2
    inv = jnp.power(ROPE_THETA, -jnp.arange(half, dtype=F32) / half)
    ang = pos.astype(F32)[:, None] * inv[None, :]
    cos = jnp.cos(ang)
    sin = jnp.sin(ang)
    cos_full = jnp.tile(jnp.concatenate([cos, cos], axis=-1), (1, MOBA_HEADS))
    sin_signed = jnp.tile(jnp.concatenate([-sin, sin], axis=-1), (1, MOBA_HEADS))
    return cos_full, sin_signed


def _top_rows(gate, n_sel):
    idx = _iota(gate.shape, 0)
    sel = jnp.zeros(gate.shape, F32)
    cur = gate
    for _ in range(n_sel):
        m = jnp.max(cur, axis=0, keepdims=True)
        first = jnp.min(jnp.where(cur == m, idx, gate.shape[0]), axis=0, keepdims=True)
        one = jnp.logical_and(idx == first, m > NEG_INF)
        sel = jnp.where(one, 1.0, sel)
        cur = jnp.where(one, NEG_INF, cur)
    return sel


def _moba_prompt_kernel(qt_ref, k_ref, vt_ref, km_ref, o_ref, sel_scr, *, group):
    qi = pl.program_id(2)
    n_heads = qt_ref.shape[1]
    d, tq = qt_ref.shape[2], qt_ref.shape[3]
    nb = km_ref.shape[2]
    blk = k_ref.shape[3]

    qs, init = [], []
    for hh in range(n_heads):
        q = qt_ref[0, hh]
        gate = _dot(km_ref[0, hh], q)
        gate = jnp.where(_iota(gate.shape, 0) < qi, gate, NEG_INF)
        sel = _top_rows(gate, min(MOBA_TOPK, nb))
        for r in range(nb):
            sel_scr[hh, r] = sel[r:r + 1, :]
        q = (q.astype(F32) * (MOBA_HEAD_DIM ** -0.5)).astype(BF16)
        qs.append(q)
        s = _dot(k_ref[0, hh, qi], q)
        s = jnp.where(_iota(s.shape, 0) <= _iota(s.shape, 1), s, NEG_INF)
        m0 = jnp.max(s, axis=0, keepdims=True)
        p = jnp.exp(s - m0)
        l0 = jnp.sum(p, axis=0, keepdims=True)
        init.append((m0, l0, _dot(vt_ref[0, hh, qi], p.astype(BF16))))

    def body(jg, carry):
        j0 = pl.multiple_of(jg * group, group)
        units = [(hh, g) for g in range(group) for hh in range(n_heads)]
        state = list(carry)
        score = lambda hh, g: _dot(k_ref[0, hh, j0 + g], qs[hh])
        s_next = score(*units[0])
        for u, (hh, g) in enumerate(units):
            s = s_next
            if u + 1 < len(units):
                s_next = score(*units[u + 1])
            s = jnp.where(sel_scr[hh, j0 + g] > 0.5, s, NEG_INF)
            m_u = jnp.max(s, axis=0, keepdims=True)
            m_ref = jnp.where(m_u > NEG_INF, m_u, 0.0)
            p = jnp.exp(s - m_ref)
            l_u = jnp.sum(p, axis=0, keepdims=True)
            pv = _dot(vt_ref[0, hh, j0 + g], p.astype(BF16))
            m, l, acc = state[hh]
            m_new = jnp.maximum(m, m_u)
            w_old = jnp.exp(m - m_new)
            w_u = jnp.exp(m_u - m_new)
            state[hh] = (m_new, l * w_old + l_u * w_u, acc * w_old + pv * w_u)
        return tuple(state)

    final = lax.fori_loop(0, (qi + group - 1) // group, body, tuple(init))
    for hh in range(n_heads):
        _, l, acc = final[hh]
        o_ref[0, hh] = (acc / l).astype(BF16)


def _moba_prompt(qt_hm, k_blk, vt_blk, km_hm):
    bsz, nh, d, seq = qt_hm.shape
    nb = k_blk.shape[2]
    tq = MOBA_BLOCK
    group = math.gcd(nb, 4)
    hps = 2
    return pl.pallas_call(
        functools.partial(_moba_prompt_kernel, group=group),
        name="moba_prompt",
        out_shape=jax.ShapeDtypeStruct((bsz, nh, d, seq), BF16),
        grid=(bsz, nh // hps, seq // tq),
        in_specs=[pl.BlockSpec((1, hps, d, tq), lambda b, h, i: (b, h, 0, i)),
                  pl.BlockSpec((1, hps, nb, MOBA_BLOCK, d), lambda b, h, i: (b, h, 0, 0, 0)),
                  pl.BlockSpec((1, hps, nb, d, MOBA_BLOCK), lambda b, h, i: (b, h, 0, 0, 0)),
                  pl.BlockSpec((1, hps, nb, d), lambda b, h, i: (b, h, 0, 0))],
        out_specs=pl.BlockSpec((1, hps, d, tq), lambda b, h, i: (b, h, 0, i)),
        scratch_shapes=[pltpu.VMEM((hps, nb, 1, tq), F32)],
        compiler_params=_cparams("arbitrary", "arbitrary", "arbitrary"),
    )(qt_hm, k_blk, vt_blk, km_hm)


def _bdot(a, b):
    return lax.dot_general(a, b, (((2,), (1,)), ((0,), (0,))), preferred_element_type=F32)


def _bdot_nt(a, b):
    return lax.dot_general(a, b, (((2,), (2,)), ((0,), (0,))), preferred_element_type=F32)


def _moba_sample_kernel(pt_ref, q_ref, kn_ref, vn_ref, *rest, n_new, bps):
    page_refs = rest[:4 * bps]
    o_ref, o_scr, m_scr, l_scr, ks_scr = rest[4 * bps:]
    j = pl.program_id(1)
    nblk = pl.num_programs(1)
    q = q_ref[0]
    nh, rows, d = q.shape
    qs = (q.astype(F32) * (MOBA_HEAD_DIM ** -0.5)).astype(BF16)

    ones = jnp.ones((nh, rows, PAGE_SIZE), BF16)
    for blk in range(bps):
        kt = [page_refs[2 * blk + a][0] for a in range(2)]
        vt = [page_refs[2 * bps + 2 * blk + a][0] for a in range(2)]
        s = jnp.concatenate([_bdot(qs, k.astype(BF16)) for k in kt], axis=-1)
        m = jnp.max(s, axis=-1, keepdims=True)
        p = jnp.exp(s - m)
        l = jnp.sum(p, axis=-1, keepdims=True)
        pb = p.astype(BF16)
        o = (_bdot_nt(pb[:, :, :PAGE_SIZE], vt[0].astype(BF16))
             + _bdot_nt(pb[:, :, PAGE_SIZE:], vt[1].astype(BF16)))
        slot = j * bps + blk
        o_scr[slot] = o
        m_scr[slot] = jnp.broadcast_to(m, o.shape)
        l_scr[slot] = jnp.broadcast_to(l, o.shape)
        ksum = None
        for k in kt:
            hi = k.astype(BF16)
            mid = (k - hi.astype(F32)).astype(BF16)
            part = _bdot_nt(ones, hi) + _bdot_nt(ones, mid)
            ksum = part if ksum is None else ksum + part
        ks_scr[slot] = ksum

    @pl.when(j == nblk - 1)
    def _():
        nb = ks_scr.shape[0]
        kmean = (ks_scr[...] * (1.0 / MOBA_BLOCK)).astype(BF16).astype(F32)
        gate = jnp.sum(q.astype(F32)[None] * kmean, axis=-1, keepdims=True)
        sel = _top_rows(jnp.broadcast_to(gate, kmean.shape), min(MOBA_TOPK, nb)) > 0.5

        s_own = _bdot_nt(qs, kn_ref[0])
        causal = _iota(s_own.shape, 2) <= _iota(s_own.shape, 1)
        s_own = jnp.where(jnp.logical_and(causal, _iota(s_own.shape, 2) < n_new), s_own, NEG_INF)
        m_o = jnp.max(s_own, axis=-1, keepdims=True)
        p_o = jnp.exp(s_own - m_o)
        l_o = jnp.sum(p_o, axis=-1, keepdims=True)
        o_o = _bdot(p_o.astype(BF16), vn_ref[0])

        m_all = m_scr[...]
        m_top = jnp.maximum(m_o, jnp.max(jnp.where(sel, m_all, NEG_INF), axis=0))
        w = jnp.where(sel, jnp.exp(m_all - m_top[None]), 0.0)
        w_o = jnp.exp(m_o - m_top)
        l_tot = l_o * w_o + jnp.sum(w * l_scr[...], axis=0)
        o_tot = o_o * w_o + jnp.sum(w * o_scr[...], axis=0)
        o_ref[0] = o_tot / l_tot


def _moba_sample(page_table, q_hm, kn_hm, vn_hm, cache_kt, cache_vt, *, n_new):
    bsz, nh, rows, d = q_hm.shape
    n_pages = page_table.shape[1]
    pages_per_blk = MOBA_BLOCK // PAGE_SIZE
    assert pages_per_blk == 2 and n_pages % pages_per_blk == 0
    nblk = n_pages // pages_per_blk
    bps = math.gcd(nblk, 2)
    pps = bps * pages_per_blk

    def page_spec(a):
        return pl.BlockSpec((1, nh, d, PAGE_SIZE), lambda b, j, pt: (pt[b * n_pages + pps * j + a], 0, 0, 0))

    seq_spec = pl.BlockSpec((1, nh, rows, d), lambda b, j, pt: (b, 0, 0, 0))
    part = pltpu.VMEM((nblk, nh, rows, d), F32)
    page_specs = [page_spec(a) for a in range(pps)]
    grid_spec = pltpu.PrefetchScalarGridSpec(
        num_scalar_prefetch=1,
        grid=(bsz, nblk // bps),
        in_specs=[seq_spec, seq_spec, seq_spec] + page_specs + page_specs,
        out_specs=seq_spec,
        scratch_shapes=[part, part, part, part])
    return pl.pallas_call(
        functools.partial(_moba_sample_kernel, n_new=n_new, bps=bps),
        name="moba_sample",
        out_shape=jax.ShapeDtypeStruct((bsz, nh, rows, d), F32),
        grid_spec=grid_spec,
        compiler_params=_cparams("arbitrary", "arbitrary"),
    )(page_table.reshape(-1), q_hm, kn_hm, vn_hm, *([cache_kt] * pps), *([cache_vt] * pps))


def _ssd_kernel(z_ref, xs_ref, bm_ref, cm_ref, dt_ref, conv0_ref, st0_ref, cw_ref, cb_ref, dtb_ref,
                alog_ref, dskip_ref, ng_ref, y_ref, stout_ref, convout_ref, st_scr, xbuf, yd_scr, *, lv):
    c = pl.program_id(1)
    last = pl.num_programs(1) - 1
    q = SSD_CHUNK
    hp = SSD_INNER
    gw = hp // SSD_GROUPS

    @pl.when(c == 0)
    def _():
        st_scr[...] = st0_ref[0]
        xbuf[0:8, :] = conv0_ref[0]

    def pad(x):
        if lv == q:
            return x
        return jnp.concatenate([x, jnp.zeros((q - lv, x.shape[1]), x.dtype)], axis=0)

    xbuf[8:8 + q, 0:hp] = pad(xs_ref[...])
    xbuf[8:8 + q, hp:hp + SSD_BC] = pad(bm_ref[...])
    xbuf[8:8 + q, hp + SSD_BC:] = pad(cm_ref[...])
    first = 8 - (SSD_CONV - 1)
    conv = cb_ref[...] + xbuf[first:first + q, :] * cw_ref[0:1, :]
    for tap in range(1, SSD_CONV):
        conv = conv + xbuf[first + tap:first + tap + q, :] * cw_ref[tap:tap + 1, :]

    @pl.when(c == last)
    def _():
        convout_ref[0] = xbuf[8 + lv - (SSD_CONV - 1):8 + lv, :]

    xbuf[0:8, :] = xbuf[q:q + 8, :]

    xbc = jax.nn.silu(conv)
    xs = xbc[:, :hp]
    bmat = xbc[:, hp:hp + SSD_BC]
    cmat = xbc[:, hp + SSD_BC:]

    dt = jax.nn.softplus(pad(dt_ref[...]) + dtb_ref[...])
    if lv != q:
        dt = jnp.where(_iota(dt.shape, 0) < lv, dt, 0.0)
    adt = dt * (-jnp.exp(alog_ref[...]))
    tri = (_iota((q, q), 1) <= _iota((q, q), 0)).astype(BF16)
    hi, mid, lo = _split3(adt)
    a_cum = _dot(tri, hi) + _dot(tri, mid) + _dot(tri, lo)
    a_cum_t = a_cum.T

    head_of_col = _iota((LANES, hp), 1) // SSD_HEAD_DIM
    e_cols = (head_of_col == _iota((LANES, hp), 0)).astype(BF16)
    e_time = (_iota((LANES, SSD_HEADS * q), 1) // q == _iota((LANES, SSD_HEADS * q), 0)).astype(BF16)
    dt_full = _dot_exact_rhs01(dt, e_cols)
    a_full = _dot_exact_rhs01(a_cum, e_cols)
    a_col = _dot_exact_rhs01(a_cum, e_time)
    a_last = a_full[q - 1:q, :]

    xdt = xs * dt_full
    xdt_b = xdt.astype(BF16)
    xdt_end = (xdt * jnp.exp(a_last - a_full)).astype(BF16)
    grow = jnp.exp(a_full)
    carry = jnp.exp(a_last)
    causal = _iota((q, q), 1) <= _iota((q, q), 0)
    low_half = _iota((q, LANES), 1) < SSD_HEAD_DIM
    heads_per_group = SSD_HEADS // SSD_GROUPS

    for g in range(SSD_GROUPS):
        cols = slice(g * gw, (g + 1) * gw)
        cc = cmat[:, g * SSD_STATE:(g + 1) * SSD_STATE].astype(BF16)
        bc = bmat[:, g * SSD_STATE:(g + 1) * SSD_STATE]
        cb = _dot_nt(cc, bc.astype(BF16))
        st_g = st_scr[:, cols]
        y_off = _dot(cc, st_g.astype(BF16)) * grow[:, cols]
        st_scr[:, cols] = st_g * carry[:, cols] + _dot(bc.T.astype(BF16), xdt_end[:, cols])
        for pair in range(heads_per_group // 2):
            h0 = g * heads_per_group + 2 * pair
            pcols = slice(h0 * SSD_HEAD_DIM, (h0 + 2) * SSD_HEAD_DIM)
            outs = []
            for h in (h0, h0 + 1):
                seg = a_col[:, h * q:(h + 1) * q] - a_cum_t[h:h + 1, :]
                decay = jnp.exp(jnp.where(causal, seg, NEG_INF))
                outs.append(_dot((cb * decay).astype(BF16), xdt_b[:, pcols]))
            yd_scr[:, pcols] = jnp.where(low_half, outs[0], outs[1]) + y_off[:, pcols.start - g * gw:pcols.stop - g * gw]

    y = yd_scr[...] + dskip_ref[...] * xs
    y = y * jax.nn.silu(pad(z_ref[...]))
    y = y * lax.rsqrt(jnp.mean(y * y, axis=-1, keepdims=True) + EPS) * ng_ref[...]
    y_ref[...] = y[:lv].astype(y_ref.dtype)

    @pl.when(c == last)
    def _():
        stout_ref[0] = st_scr[...]


def _ssd_scan(proj, conv0p, st0_t, conv_w, conv_b, dt_bias, a_log, dskip_full, norm_g, *, bsz, seq_len):
    lv = min(seq_len, SSD_CHUNK)
    nc = seq_len // lv
    hp = SSD_INNER
    t = proj.shape[0]
    blk = lambda w, cidx: pl.BlockSpec((lv, w), lambda b, c: (b * nc + c, cidx))
    full = lambda shape: pl.BlockSpec(shape, lambda b, c: (0,) * len(shape))
    per_seq = lambda shape: pl.BlockSpec((1,) + shape, lambda b, c: (b,) + (0,) * len(shape))
    return pl.pallas_call(
        functools.partial(_ssd_kernel, lv=lv),
        name="ssd_scan",
        out_shape=[jax.ShapeDtypeStruct((t, hp), BF16 if lv % 16 == 0 else F32),
                   jax.ShapeDtypeStruct((bsz, SSD_STATE, hp), F32),
                   jax.ShapeDtypeStruct((bsz, SSD_CONV - 1, SSD_CONV_DIM), F32)],
        grid=(bsz, nc),
        in_specs=[blk(hp, 0), blk(hp, 1), blk(SSD_BC, 2 * hp // SSD_BC), blk(SSD_BC, 2 * hp // SSD_BC + 1),
                  blk(LANES, (2 * hp + 2 * SSD_BC) // LANES),
                  per_seq((8, SSD_CONV_DIM)), per_seq((SSD_STATE, hp)),
                  full((SSD_CONV, SSD_CONV_DIM)), full((1, SSD_CONV_DIM)), full((1, LANES)), full((1, LANES)),
                  full((1, hp)), full((1, hp))],
        out_specs=[pl.BlockSpec((lv, hp), lambda b, c: (b * nc + c, 0)),
                   per_seq((SSD_STATE, hp)), per_seq((SSD_CONV - 1, SSD_CONV_DIM))],
        scratch_shapes=[pltpu.VMEM((SSD_STATE, hp), F32), pltpu.VMEM((SSD_CHUNK + 8, SSD_CONV_DIM), F32),
                        pltpu.VMEM((SSD_CHUNK, hp), F32)],
        compiler_params=_cparams("arbitrary", "arbitrary"),
    )(proj, proj, proj, proj, proj, conv0p, st0_t, conv_w, conv_b.reshape(1, -1), dt_bias, a_log,
      dskip_full, norm_g.reshape(1, -1))


def _argmax_rows(x):
    rows, n = x.shape
    idx = _iota((rows, n), 0)
    vs = [x[r:r + 8] for r in range(0, rows, 8)]
    ids = [idx[r:r + 8] for r in range(0, rows, 8)]
    while len(vs) > 1:
        nv, ni = [], []
        for p in range(0, len(vs) - 1, 2):
            nv.append(jnp.maximum(vs[p], vs[p + 1]))
            ni.append(jnp.where(vs[p] >= vs[p + 1], ids[p], ids[p + 1]))
        if len(vs) % 2:
            nv.append(vs[-1])
            ni.append(ids[-1])
        vs, ids = nv, ni
    m = jnp.max(vs[0], axis=0, keepdims=True)
    first = jnp.min(jnp.where(vs[0] == m, ids[0], rows), axis=0, keepdims=True)
    return m, first


def _extract_top(cur_scr, n_chain, k):
    rows, n = cur_scr.shape[1], cur_scr.shape[2]
    idx = _iota((rows, n), 0)
    slot = _iota((k, n), 0)

    def body(i, carry):
        out = []
        for c in range(n_chain):
            vals, picks = carry[c]
            cur = cur_scr[c]
            m, first = _argmax_rows(cur)
            cur_scr[c] = jnp.where(idx == first, NEG_INF, cur)
            out.append((jnp.where(slot == i, m, vals), jnp.where(slot == i, first, picks)))
        return tuple(out)

    init = tuple((jnp.zeros((k, n), F32), jnp.zeros((k, n), jnp.int32)) for _ in range(n_chain))
    return lax.fori_loop(0, k, body, init)


def _candidate_pieces(k):
    assert k == 16
    pieces = [(0, 1, 16, 16), (1, 2, 8, 8)]
    pieces += [(a, a + 1, 8, k // (a + 1)) for a in range(2, 8)]
    pieces.append((8, 16, 1, 1))
    return pieces


def _peer_route_kernel(qt_ref, sk_ref, alpha_ref, count_ref, beta_ref, rank_ref, cur_scr, cand_scr):
    k = PEER_TOPK
    hd = sk_ref.shape[3]
    nk = sk_ref.shape[2]
    pieces = _candidate_pieces(k)
    n_lt = qt_ref.shape[1] // LANES
    idx = _iota((nk, LANES), 0)

    scores = []
    for lt in range(n_lt):
        lanes = slice(lt * LANES, (lt + 1) * LANES)
        for c in range(2):
            s = _dot(sk_ref[0, c], qt_ref[c * hd:(c + 1) * hd, lanes])
            cur_scr[2 * lt + c] = s
            scores.append(s)
    tops = _extract_top(cur_scr, 2 * n_lt, k)

    for lt in range(n_lt):
        v1, v2 = tops[2 * lt][0], tops[2 * lt + 1][0]
        cands = []
        for a_lo, a_hi, n_b, n_valid in pieces:
            if a_hi - a_lo == 1:
                c = v1[a_lo:a_hi, :] + v2[0:n_b, :]
                cands.append(jnp.where(_iota(c.shape, 0) < n_valid, c, NEG_INF))
            else:
                cands.append(v1[a_lo:a_hi, :] + v2[0:1, :])
        cand_scr[lt] = jnp.concatenate(cands, axis=0)
    picked = _extract_top(cand_scr, n_lt, k)

    for lt in range(n_lt):
        lanes = slice(lt * LANES, (lt + 1) * LANES)
        (v1, rows1), (v2, rows2) = tops[2 * lt], tops[2 * lt + 1]
        best, best_rows = picked[lt]
        z = jnp.sum(jnp.exp(best - best[0:1, :]), axis=0, keepdims=True)
        count = jnp.zeros((nk, LANES), F32)
        rank2 = jnp.full((nk, LANES), float(k), F32)
        row = 0
        for a_lo, a_hi, n_b, _ in pieces:
            for a in range(a_lo, a_hi):
                in_piece = jnp.logical_and(best_rows >= row, best_rows < row + n_b)
                n_a = jnp.sum(jnp.where(in_piece, 1.0, 0.0), axis=0, keepdims=True)
                count = jnp.where(idx == rows1[a:a + 1, :], n_a, count)
                row += n_b
        for b in range(k):
            rank2 = jnp.where(idx == rows2[b:b + 1, :], float(b), rank2)
        alpha_ref[0, :, lanes] = jnp.exp(scores[2 * lt] - v1[0:1, :]) / z
        count_ref[0, :, lanes] = count
        beta_ref[0, :, lanes] = jnp.exp(scores[2 * lt + 1] - v2[0:1, :])
        rank_ref[0, :, lanes] = rank2


def _peer_route(q_t, subkeys, *, tt):
    t = q_t.shape[1]
    nh, _, nk, hd = subkeys.shape
    out = jax.ShapeDtypeStruct((nh, nk, t), F32)
    ospec = pl.BlockSpec((1, nk, tt), lambda i, h: (h, 0, i))
    n_cand = sum((a_hi - a_lo) * n_b for a_lo, a_hi, n_b, _ in _candidate_pieces(PEER_TOPK))
    return pl.pallas_call(
        _peer_route_kernel,
        name="peer_route",
        out_shape=[out, out, out, out],
        grid=(t // tt, nh),
        in_specs=[pl.BlockSpec((2 * hd, tt), lambda i, h: (h, i)),
                  pl.BlockSpec((1, 2, nk, hd), lambda i, h: (h, 0, 0, 0))],
        out_specs=[ospec, ospec, ospec, ospec],
        scratch_shapes=[pltpu.VMEM((2 * (tt // LANES), nk, LANES), F32),
                        pltpu.VMEM((tt // LANES, n_cand, LANES), F32)],
        compiler_params=_cparams("arbitrary", "arbitrary"),
    )(q_t, subkeys)


def _peer_dense_kernel(ht_ref, u_ref, vt_ref, alpha_ref, count_ref, beta_ref, rank_ref, o_ref, aw_scr):
    e = pl.program_id(1)
    nh, nk, tt = beta_ref.shape
    n_sub, _, sub = vt_ref.shape
    per_sub = sub // nk

    @pl.when(e == 0)
    def _():
        o_ref[...] = jnp.zeros(o_ref.shape, F32)

    s_next = _dot(u_ref[0:sub, :], ht_ref[...])
    acc = None
    for c in range(n_sub):
        s_cur = s_next
        if c + 1 < n_sub:
            s_next = _dot(u_ref[(c + 1) * sub:(c + 2) * sub, :], ht_ref[...])
        for lt in range(tt // LANES):
            lanes = slice(lt * LANES, (lt + 1) * LANES)
            for kk in range(per_sub):
                rows = slice(kk * nk, (kk + 1) * nk)
                w = jnp.zeros((nk, LANES), F32)
                for h in range(nh):
                    hit = rank_ref[h, :, lanes] < count_ref[h, c, kk:kk + 1, lanes]
                    w = w + jnp.where(hit, beta_ref[h, :, lanes], 0.0) * alpha_ref[h, c, kk:kk + 1, lanes]
                aw_scr[c, rows, lanes] = (w * _gelu(s_cur[rows, lanes])).astype(BF16)
        part = _dot(vt_ref[c], aw_scr[c])
        acc = part if acc is None else acc + part
    o_ref[...] += acc


def _peer_dense(h_t, u_tabs, vt_tiles, layer, route, *, tt, tile_subs):
    d, t = h_t.shape
    alpha, count, beta, rank = route
    nh, nk, _ = alpha.shape
    sub = vt_tiles.shape[3]
    per_sub = sub // nk
    te = tile_subs * sub
    alpha = alpha.reshape(nh, nk // per_sub, per_sub, t)
    count = count.reshape(nh, nk // per_sub, per_sub, t)
    rspec = pl.BlockSpec((nh, nk, tt), lambda i, e: (0, 0, i))
    fspec = pl.BlockSpec((nh, tile_subs, per_sub, tt), lambda i, e: (0, e, 0, i))
    return pl.pallas_call(
        _peer_dense_kernel,
        name="peer_dense",
        out_shape=jax.ShapeDtypeStruct((d, t), F32),
        grid=(t // tt, nk * nk // te),
        in_specs=[pl.BlockSpec((d, tt), lambda i, e: (0, i)),
                  pl.BlockSpec((None, te, d), lambda i, e: (layer, e, 0)),
                  pl.BlockSpec((None, tile_subs, d, sub), lambda i, e: (layer, e, 0, 0)),
                  fspec, fspec, rspec, rspec],
        out_specs=pl.BlockSpec((d, tt), lambda i, e: (0, i)),
        scratch_shapes=[pltpu.VMEM((tile_subs, sub, tt), BF16)],
        compiler_params=_cparams("arbitrary", "arbitrary"),
    )(h_t, u_tabs, vt_tiles, alpha, count, beta, rank)


def _prep_weights(norm_mix_g, mod_mix_w, mod_mix_b, norm_ffn_g, mod_ffn_w, mod_ffn_b,
                  w_in_ab, w_out_ab, gmlp_ln_g, gmlp_ln_b, gmlp_ws, gmlp_bs,
                  ssd_w_in, ssd_conv_w, ssd_conv_b, ssd_dt_bias, ssd_a_log, ssd_d, ssd_norm_g, ssd_w_out,
                  peer_wq, peer_subkeys, peer_u, peer_v, final_norm_g):
    depth = norm_mix_g.shape[0]
    d = norm_mix_g.shape[1]
    p = dict(depth=depth, norm_mix_g=norm_mix_g, norm_ffn_g=norm_ffn_g, final_norm_g=final_norm_g,
             gmlp_ln_g=gmlp_ln_g, gmlp_ln_b=gmlp_ln_b, gmlp_ws=gmlp_ws, gmlp_bs=gmlp_bs,
             ssd_conv_w=ssd_conv_w, ssd_conv_b=ssd_conv_b, ssd_norm_g=ssd_norm_g)
    mod_w, mod_b = [], []
    for i in range(depth):
        mod_w += [mod_mix_w[i], mod_ffn_w[i]]
        mod_b += [mod_mix_b[i], mod_ffn_b[i]]
    p["mod_w"] = jnp.stack(mod_w).astype(BF16)
    p["mod_b"] = jnp.stack(mod_b)[:, None, :]
    p["w_in_ab"] = w_in_ab.astype(BF16)
    p["w_out_ab"] = w_out_ab.astype(BF16)
    n_ssd = ssd_w_in.shape[0]
    z_w = ssd_w_in[:, :, :SSD_INNER]
    xbc_w = ssd_w_in[:, :, SSD_INNER:SSD_INNER + SSD_CONV_DIM]
    dt_w = ssd_w_in[:, :, SSD_INNER + SSD_CONV_DIM:]
    used = 2 * SSD_INNER + 2 * SSD_BC + LANES
    total = -(-used // 1792) * 1792
    p["ssd_w_in"] = jnp.concatenate(
        [z_w, xbc_w, dt_w, jnp.zeros((n_ssd, d, total - used + LANES - SSD_HEADS), F32)], axis=-1).astype(BF16)
    pad_heads = lambda a, v: jnp.concatenate([a, jnp.full((n_ssd, LANES - SSD_HEADS), v, F32)], axis=-1)[:, None, :]
    p["ssd_dt_bias"] = pad_heads(ssd_dt_bias, 0.0)
    p["ssd_a_log"] = pad_heads(ssd_a_log, 0.0)
    p["ssd_dskip"] = jnp.repeat(ssd_d, SSD_HEAD_DIM, axis=-1)[:, None, :]
    p["ssd_w_out"] = ssd_w_out.astype(BF16)
    p["peer_wq"] = peer_wq.astype(BF16)
    p["peer_subkeys"] = peer_subkeys.astype(BF16)
    p["peer_u"] = peer_u.astype(BF16)
    n_keys = peer_subkeys.shape[3]
    sub = PEER_FIRST_KEYS_PER_SUBTILE * n_keys
    p["peer_vt"] = peer_v.astype(BF16).reshape(depth, n_keys * n_keys // sub, sub, d).transpose(0, 1, 3, 2)
    return p


def _peer_layer(x, mod, seq_len, p, i, *, tm, final):
    q, h = _adaln_matmul(x, p["norm_ffn_g"][i], mod, seq_len, p["peer_wq"][i], tm=tm,
                         tn=p["peer_wq"].shape[2] // 2, out_dtype=BF16, emit_h=True)
    nk = p["peer_subkeys"].shape[3]
    route = _peer_route(q.T, p["peer_subkeys"][i], tt=min(256, tm))
    y_t = _peer_dense(h.T, p["peer_u"], p["peer_vt"], i, route, tt=min(256, tm),
                      tile_subs=PEER_SUBTILES_PER_STEP)
    return _residual_t(x, y_t, mod, seq_len, tm=tm, final_g=p["final_norm_g"] if final else None)


def _ab_layer(x, mod, bsz, seq_len, pos, past, p, i, j, *, tm):
    t, d = x.shape
    w = GMLP_GROUPS * (d // 16)
    proj = _adaln_matmul(x, p["norm_mix_g"][i], mod, seq_len, p["w_in_ab"][j], tm=tm,
                         tn=p["w_in_ab"].shape[2] // 2)
    cos, sin = _rope_tables(pos)
    lc = min(seq_len, GMLP_CHUNK)
    ws = p["gmlp_ws"][j][:, :lc, :lc]
    bias = jnp.repeat(p["gmlp_bs"][j][:, :lc].T, w // GMLP_GROUPS, axis=1)
    if seq_len < GMLP_CHUNK:
        reps = tm // seq_len
        ws = jnp.where(jnp.tril(jnp.ones((lc, lc), bool)), ws, 0.0)
        eye = jnp.eye(reps, dtype=F32)
        ws = jnp.einsum("ab,gts->gatbs", eye, ws).reshape(GMLP_GROUPS, tm, tm)
        bias = jnp.tile(bias, (reps, 1))
        cos = jnp.tile(cos, (reps, 1))
        sin = jnp.tile(sin, (reps, 1))
        tr, lc_eff = tm, tm
    else:
        tr, lc_eff = MOBA_BLOCK, lc
    a_out, vln, q_rot, k_rot, kmean = _ab_post(proj, cos, sin, p["gmlp_ln_g"][j], p["gmlp_ln_b"][j],
                                               ws.astype(BF16), bias, tr=tr, lc=lc_eff)
    v_new = proj[:, 2 * w + 2 * w:2 * w + 3 * w]
    nh, hd = MOBA_HEADS, MOBA_HEAD_DIM
    if past is None:
        nb = seq_len // MOBA_BLOCK
        qt_hm = q_rot.reshape(bsz, seq_len, nh, hd).transpose(0, 2, 3, 1)
        k_blk = k_rot.astype(BF16).reshape(bsz, nb, MOBA_BLOCK, nh, hd).transpose(0, 3, 1, 2, 4)
        vt_blk = v_new.astype(BF16).reshape(bsz, nb, MOBA_BLOCK, nh, hd).transpose(0, 3, 1, 4, 2)
        km_hm = kmean.reshape(bsz, nb, nh, hd).transpose(0, 2, 1, 3).astype(BF16)
        o_t = _moba_prompt(qt_hm, k_blk, vt_blk, km_hm)
        b_out = o_t.transpose(0, 3, 1, 2).reshape(t, w)
    else:
        page_table, cache_k, cache_v = past
        rows = -(-seq_len // 16) * 16

        def head_major(a):
            a = a.astype(BF16).reshape(bsz, seq_len, nh, hd).transpose(0, 2, 1, 3)
            return jnp.pad(a, ((0, 0), (0, 0), (0, rows - seq_len), (0, 0)))

        b_out = _moba_sample(page_table, head_major(q_rot), head_major(k_rot), head_major(v_new),
                             cache_k.transpose(0, 2, 3, 1), cache_v.transpose(0, 2, 3, 1), n_new=seq_len)
        b_out = b_out[:, :, :seq_len].transpose(0, 2, 1, 3).reshape(t, w).astype(BF16)
    x = _proj_residual([a_out, b_out], p["w_out_ab"][j], x, mod, seq_len, tm=tm)
    lcv = min(seq_len, GMLP_CHUNK)
    gv = vln.reshape(bsz, seq_len, w)[:, seq_len - lcv:]
    return (x, k_rot.reshape(bsz, seq_len, nh, hd), v_new.reshape(bsz, seq_len, nh, hd), gv)


def _ssd_layer(x, mod, bsz, seq_len, ssm0, conv0, p, i, j, *, tm):
    t, d = x.shape
    proj = _adaln_matmul(x, p["norm_mix_g"][i], mod, seq_len, p["ssd_w_in"][j], tm=tm, tn=1792)
    if conv0 is None:
        conv0 = jnp.zeros((bsz, SSD_CONV - 1, SSD_CONV_DIM), F32)
    if ssm0 is None:
        st0_t = jnp.zeros((bsz, SSD_STATE, SSD_INNER), F32)
    else:
        st0_t = ssm0.reshape(bsz, SSD_INNER, SSD_STATE).transpose(0, 2, 1)
    conv0p = jnp.pad(conv0, ((0, 0), (8 - (SSD_CONV - 1), 0), (0, 0)))
    y, st_t, conv_new = _ssd_scan(proj, conv0p, st0_t, p["ssd_conv_w"][j], p["ssd_conv_b"][j],
                                  p["ssd_dt_bias"][j], p["ssd_a_log"][j], p["ssd_dskip"][j],
                                  p["ssd_norm_g"][j], bsz=bsz, seq_len=seq_len)
    x = _proj_residual([y], p["ssd_w_out"][j], x, mod, seq_len, tm=tm)
    ssm_new = st_t.transpose(0, 2, 1).reshape(bsz, SSD_HEADS, SSD_HEAD_DIM, SSD_STATE)
    return x, ssm_new, conv_new


def _run_trunk(x, c, pos0, past_kv, ssm_past, conv_past, p):
    bsz, seq_len, d = x.shape
    t = bsz * seq_len
    tm = next(c for c in (1024, 512, 256) if (t % c == 0 and seq_len % c == 0) or c == 256)
    assert t % tm == 0
    x = x.reshape(t, d)
    pos = pos0 + jnp.arange(seq_len, dtype=jnp.int32)
    mods = _mod_all(c, p["mod_w"], p["mod_b"])
    depth = p["depth"]
    k_rows, v_rows, gv_rows, ssm_new, conv_new = [], [], [], [], []
    for i in range(depth):
        j = i // 2
        if i % 2 == 0:
            past = None if past_kv is None else (past_kv[0], past_kv[1][j], past_kv[2][j])
            x, k_new, v_new, gv = _ab_layer(x, mods[2 * i], bsz, seq_len, pos, past, p, i, j, tm=tm)
            k_rows.append(k_new)
            v_rows.append(v_new)
            gv_rows.append(gv)
        else:
            x, s_new, cv_new = _ssd_layer(x, mods[2 * i], bsz, seq_len,
                                          None if ssm_past is None else ssm_past[j],
                                          None if conv_past is None else conv_past[j], p, i, j, tm=tm)
            ssm_new.append(s_new)
            conv_new.append(cv_new)
        x = _peer_layer(x, mods[2 * i + 1], seq_len, p, i, tm=tm, final=(i == depth - 1))
    y = x.reshape(bsz, seq_len, d)
    return y, jnp.stack(k_rows), jnp.stack(v_rows), jnp.stack(gv_rows), jnp.stack(ssm_new), jnp.stack(conv_new)


def kernel(x_prompt, x_sample, cache_k, cache_v, page_table, state_ssm, state_conv, c_prompt, c_sample,
           norm_mix_g, mod_mix_w, mod_mix_b, norm_ffn_g, mod_ffn_w, mod_ffn_b,
           w_in_ab, w_out_ab, gmlp_ln_g, gmlp_ln_b, gmlp_ws, gmlp_bs,
           ssd_w_in, ssd_conv_w, ssd_conv_b, ssd_dt_bias, ssd_a_log, ssd_d, ssd_norm_g, ssd_w_out,
           peer_wq, peer_subkeys, peer_u, peer_v, final_norm_g):
    p = _prep_weights(norm_mix_g, mod_mix_w, mod_mix_b, norm_ffn_g, mod_ffn_w, mod_ffn_b,
                      w_in_ab, w_out_ab, gmlp_ln_g, gmlp_ln_b, gmlp_ws, gmlp_bs,
                      ssd_w_in, ssd_conv_w, ssd_conv_b, ssd_dt_bias, ssd_a_log, ssd_d, ssd_norm_g, ssd_w_out,
                      peer_wq, peer_subkeys, peer_u, peer_v, final_norm_g)
    prompt = _run_trunk(x_prompt, c_prompt, 0, None, None, None, p)
    past_len = page_table.shape[1] * PAGE_SIZE
    sample = _run_trunk(x_sample, c_sample, past_len, (page_table, cache_k, cache_v),
                        state_ssm, state_conv, p)
    return (prompt[0], sample[0]) + prompt[1:] + sample[1:]
```

```python
import functools
import math

import jax
import jax.numpy as jnp
from jax import lax
from jax.experimental import pallas as pl
from jax.experimental.pallas import tpu as pltpu

F32 = jnp.float32
BF16 = jnp.bfloat16
NEG_INF = float("-inf")

EPS = 1e-6
LANES = 128
GMLP_GROUPS = 8
GMLP_CHUNK = 128
MOBA_HEADS = 8
MOBA_HEAD_DIM = 64
MOBA_BLOCK = 256
MOBA_TOPK = 3
PAGE_SIZE = 128
ROPE_THETA = 10000.0
SSD_HEADS = 32
SSD_HEAD_DIM = 64
SSD_GROUPS = 4
SSD_STATE = 128
SSD_CONV = 4
SSD_CHUNK = 128
SSD_INNER = SSD_HEADS * SSD_HEAD_DIM
SSD_BC = SSD_GROUPS * SSD_STATE
SSD_CONV_DIM = SSD_INNER + 2 * SSD_BC
PEER_HEADS = 8
PEER_TOPK = 16
PEER_FIRST_KEYS_PER_SUBTILE = 2
PEER_SUBTILES_PER_STEP = 4
VMEM_LIMIT_BYTES = 56 * 1024 * 1024


def _cparams(*sem):
    return pltpu.CompilerParams(dimension_semantics=sem, vmem_limit_bytes=VMEM_LIMIT_BYTES)


def _dot(a, b):
    return jnp.dot(a, b, preferred_element_type=F32)


def _dot_nt(a, b):
    return lax.dot_general(a, b, (((1,), (1,)), ((), ())), preferred_element_type=F32)


def _iota(shape, dim):
    return lax.broadcasted_iota(jnp.int32, shape, dim)


def _gelu(x):
    return 0.5 * x * (1.0 + lax.erf(x * (2.0 ** -0.5)))


def _split3(x):
    hi = x.astype(BF16)
    r = x - hi.astype(F32)
    mid = r.astype(BF16)
    lo = (r - mid.astype(F32)).astype(BF16)
    return hi, mid, lo


def _dot_exact_rhs01(x, e):
    hi, mid, lo = _split3(x)
    return _dot(hi, e) + _dot(mid, e) + _dot(lo, e)


def _mod_kernel(c_ref, w_ref, b_ref, o_ref):
    o_ref[0] = _dot(c_ref[...].astype(BF16), w_ref[0]) + b_ref[0]


def _mod_all(c, w4, b4):
    bsz, d = c.shape
    rows = -(-bsz // 16) * 16
    cp = jnp.pad(c, ((0, rows - bsz), (0, 0)))
    n = w4.shape[0]
    out = pl.pallas_call(
        _mod_kernel,
        name="adaln_mod",
        out_shape=jax.ShapeDtypeStruct((n, rows, 3 * d), F32),
        grid=(n, 3),
        in_specs=[pl.BlockSpec((rows, d), lambda i, j: (0, 0)),
                  pl.BlockSpec((1, d, d), lambda i, j: (i, 0, j)),
                  pl.BlockSpec((1, 1, d), lambda i, j: (i, 0, j))],
        out_specs=pl.BlockSpec((1, rows, d), lambda i, j: (i, 0, j)),
        compiler_params=_cparams("arbitrary", "arbitrary"),
    )(cp, w4, b4)
    return out[:, :bsz]


def _mod_rows(mod, seq_len, tm):
    bsz, w = mod.shape
    if seq_len % tm == 0:
        return mod[:, None, :], seq_len // tm
    assert tm % seq_len == 0 and (bsz * seq_len) % tm == 0
    return jnp.repeat(mod, seq_len, axis=0).reshape(bsz * seq_len // tm, tm, w), 1


def _adaln_mm_kernel(x_ref, g_ref, sh_ref, sc_ref, w_ref, o_ref, *rest, feature_major):
    if feature_major:
        ht_ref, h_scr = rest
    else:
        (h_scr,) = rest

    @pl.when(pl.program_id(1) == 0)
    def _():
        x = x_ref[...]
        y = x * lax.rsqrt(jnp.mean(x * x, axis=-1, keepdims=True) + EPS)
        y = y * g_ref[...]
        h = y * (1.0 + sc_ref[0]) + sh_ref[0]
        h_scr[...] = h.astype(BF16)
        if feature_major:
            ht_ref[...] = h.T.astype(BF16)

    out = _dot(h_scr[...], w_ref[...])
    o_ref[...] = (out.T if feature_major else out).astype(o_ref.dtype)


def _adaln_matmul(x, g, mod, seq_len, w, *, tm, tn, out_dtype=F32, feature_major=False):
    t, d = x.shape
    n = w.shape[1]
    assert t % tm == 0 and n % tn == 0
    mod3, tpg = _mod_rows(mod, seq_len, tm)
    r = mod3.shape[1]
    if feature_major:
        out_shape = [jax.ShapeDtypeStruct((n, t), out_dtype), jax.ShapeDtypeStruct((d, t), BF16)]
        out_specs = [pl.BlockSpec((tn, tm), lambda i, j: (j, i)), pl.BlockSpec((d, tm), lambda i, j: (0, i))]
    else:
        out_shape = [jax.ShapeDtypeStruct((t, n), out_dtype)]
        out_specs = [pl.BlockSpec((tm, tn), lambda i, j: (i, j))]
    res = pl.pallas_call(
        functools.partial(_adaln_mm_kernel, feature_major=feature_major),
        name="adaln_matmul",
        out_shape=out_shape,
        grid=(t // tm, n // tn),
        in_specs=[pl.BlockSpec((tm, d), lambda i, j: (i, 0)),
                  pl.BlockSpec((1, d), lambda i, j: (0, 0)),
                  pl.BlockSpec((1, r, d), lambda i, j: (i // tpg, 0, 0)),
                  pl.BlockSpec((1, r, d), lambda i, j: (i // tpg, 0, 1)),
                  pl.BlockSpec((d, tn), lambda i, j: (0, j))],
        out_specs=out_specs,
        scratch_shapes=[pltpu.VMEM((tm, d), BF16)],
        compiler_params=_cparams("arbitrary", "arbitrary"),
    )(x, g.reshape(1, d), mod3, mod3, w)
    return res if feature_major else res[0]


def _proj_res_kernel(*refs, ks):
    lhs = refs[:len(ks)]
    w_ref, x_ref, gate_ref, o_ref = refs[len(ks):]
    acc = None
    off = 0
    for l_ref, k in zip(lhs, ks):
        part = _dot(l_ref[...].astype(BF16), w_ref[off:off + k, :])
        acc = part if acc is None else acc + part
        off += k
    o_ref[...] = x_ref[...] + gate_ref[0] * acc


def _proj_residual(lhs, w, x, mod, seq_len, *, tm):
    t, d = x.shape
    ks = tuple(l.shape[1] for l in lhs)
    mod3, tpg = _mod_rows(mod, seq_len, tm)
    r = mod3.shape[1]
    in_specs = [pl.BlockSpec((tm, k), lambda i: (i, 0)) for k in ks]
    in_specs += [pl.BlockSpec((sum(ks), d), lambda i: (0, 0)),
                 pl.BlockSpec((tm, d), lambda i: (i, 0)),
                 pl.BlockSpec((1, r, d), lambda i: (i // tpg, 0, 2))]
    return pl.pallas_call(
        functools.partial(_proj_res_kernel, ks=ks),
        name="proj_residual",
        out_shape=jax.ShapeDtypeStruct((t, d), F32),
        grid=(t // tm,),
        in_specs=in_specs,
        out_specs=pl.BlockSpec((tm, d), lambda i: (i, 0)),
        compiler_params=_cparams("arbitrary"),
    )(*lhs, w, x, mod3)


def _res_kernel(x_ref, yt_ref, gate_ref, *rest, final_norm):
    x2 = x_ref[...] + gate_ref[0] * yt_ref[...].T
    if final_norm:
        g_ref, o_ref = rest
        o_ref[...] = x2 * lax.rsqrt(jnp.mean(x2 * x2, axis=-1, keepdims=True) + EPS) * g_ref[...]
    else:
        (o_ref,) = rest
        o_ref[...] = x2


def _residual_t(x, y_t, mod, seq_len, *, tm, final_g=None):
    t, d = x.shape
    mod3, tpg = _mod_rows(mod, seq_len, tm)
    r = mod3.shape[1]
    in_specs = [pl.BlockSpec((tm, d), lambda i: (i, 0)),
                pl.BlockSpec((d, tm), lambda i: (0, i)),
                pl.BlockSpec((1, r, d), lambda i: (i // tpg, 0, 2))]
    args = [x, y_t, mod3]
    if final_g is not None:
        in_specs.append(pl.BlockSpec((1, d), lambda i: (0, 0)))
        args.append(final_g.reshape(1, d))
    return pl.pallas_call(
        functools.partial(_res_kernel, final_norm=final_g is not None),
        name="peer_residual",
        out_shape=jax.ShapeDtypeStruct((t, d), F32),
        grid=(t // tm,),
        in_specs=in_specs,
        out_specs=pl.BlockSpec((tm, d), lambda i: (i, 0)),
        compiler_params=_cparams("arbitrary"),
    )(*args)


def _rope(x, cos, sin_signed):
    lane = _iota(x.shape, 1) % MOBA_HEAD_DIM
    half = MOBA_HEAD_DIM // 2
    width = x.shape[1]
    partner = jnp.where(lane < half, pltpu.roll(x, width - half, 1), pltpu.roll(x, half, 1))
    return x * cos + partner * sin_signed


def _ab_post_kernel(u_ref, v_ref, q_ref, k_ref, *rest, lc, n_chunks, head_major):
    if head_major:
        (vb_ref, cos_ref, sin_ref, lng_ref, lnb_ref, ws_ref, bias_ref,
         a_ref, vln_ref, kr_ref, km_ref, qt_ref, khm_ref, vt_ref) = rest
    else:
        (cos_ref, sin_ref, lng_ref, lnb_ref, ws_ref, bias_ref,
         a_ref, vln_ref, kr_ref, km_ref, qr_ref) = rest
    u = _gelu(u_ref[...])
    v = _gelu(v_ref[...])
    mu = jnp.mean(v, axis=-1, keepdims=True)
    var = jnp.mean(jnp.square(v - mu), axis=-1, keepdims=True)
    vln = (v - mu) * lax.rsqrt(var + EPS) * lng_ref[...] + lnb_ref[...]
    vln_ref[...] = vln

    width = vln.shape[1]
    lane_group = _iota((lc, width), 1) // (width // GMLP_GROUPS)
    causal = _iota((lc, lc), 1) <= _iota((lc, lc), 0)
    w_tril = [jnp.where(causal, ws_ref[g], jnp.zeros((), BF16)) for g in range(GMLP_GROUPS)]
    for c in range(n_chunks):
        rows = slice(c * lc, (c + 1) * lc)
        vc = vln[rows].astype(BF16)
        mix = jnp.zeros((lc, width), F32)
        for g in range(GMLP_GROUPS):
            mix = mix + jnp.where(lane_group == g, _dot(w_tril[g], vc), 0.0)
        a_ref[rows, :] = (u[rows] * (mix + bias_ref[...])).astype(BF16)

    cos = cos_ref[...]
    sin = sin_ref[...]
    qr = _rope(q_ref[...], cos, sin)
    kr = _rope(k_ref[...], cos, sin)
    kr_ref[...] = kr
    km_ref[0] = jnp.mean(kr, axis=0, keepdims=True)
    if head_major:
        tr = qr.shape[0]
        qt_ref[0] = qr.T.reshape(MOBA_HEADS, MOBA_HEAD_DIM, tr).astype(BF16)
        vt_ref[0, :, 0] = vb_ref[...].T.reshape(MOBA_HEADS, MOBA_HEAD_DIM, tr).astype(BF16)
        for h in range(MOBA_HEADS):
            khm_ref[0, h, 0] = kr[:, h * MOBA_HEAD_DIM:(h + 1) * MOBA_HEAD_DIM].astype(BF16)
    else:
        qr_ref[...] = qr.astype(BF16)


def _ab_post(proj, cos, sin, ln_g, ln_b, ws, bias, *, tr, lc, head_major, bsz):
    t = proj.shape[0]
    w = ln_g.shape[0]
    n_tab = cos.shape[0] // tr
    n_tiles = t // tr
    nh, hd = MOBA_HEADS, MOBA_HEAD_DIM
    col = lambda c: pl.BlockSpec((tr, w), lambda i: (i, c))
    tab = pl.BlockSpec((tr, w), lambda i: (i % n_tab, 0))
    row = pl.BlockSpec((tr, w), lambda i: (i, 0))
    params = [pl.BlockSpec((1, w), lambda i: (0, 0)),
              pl.BlockSpec((1, w), lambda i: (0, 0)),
              pl.BlockSpec((GMLP_GROUPS, lc, lc), lambda i: (0, 0, 0)),
              pl.BlockSpec((lc, w), lambda i: (0, 0))]
    out_shape = [jax.ShapeDtypeStruct((t, w), BF16),
                 jax.ShapeDtypeStruct((t, w), F32),
                 jax.ShapeDtypeStruct((t, w), F32),
                 jax.ShapeDtypeStruct((n_tiles, 1, w), F32)]
    out_specs = [row, row, row, pl.BlockSpec((1, 1, w), lambda i: (i, 0, 0))]
    if head_major:
        assert tr == MOBA_BLOCK
        in_specs = [col(0), col(1), col(2), col(3), col(4), tab, tab] + params
        args = (proj,) * 5
        out_shape += [jax.ShapeDtypeStruct((bsz, nh, hd, n_tab * tr), BF16),
                      jax.ShapeDtypeStruct((bsz, nh, n_tab, tr, hd), BF16),
                      jax.ShapeDtypeStruct((bsz, nh, n_tab, hd, tr), BF16)]
        out_specs += [pl.BlockSpec((1, nh, hd, tr), lambda i: (i // n_tab, 0, 0, i % n_tab)),
                      pl.BlockSpec((1, nh, 1, tr, hd), lambda i: (i // n_tab, 0, i % n_tab, 0, 0)),
                      pl.BlockSpec((1, nh, 1, hd, tr), lambda i: (i // n_tab, 0, i % n_tab, 0, 0))]
    else:
        in_specs = [col(0), col(1), col(2), col(3), tab, tab] + params
        args = (proj,) * 4
        out_shape.append(jax.ShapeDtypeStruct((t, w), BF16))
        out_specs.append(row)
    return pl.pallas_call(
        functools.partial(_ab_post_kernel, lc=lc, n_chunks=tr // lc, head_major=head_major),
        name="gmlp_rope",
        out_shape=out_shape,
        grid=(n_tiles,),
        in_specs=in_specs,
        out_specs=out_specs,
        compiler_params=_cparams("arbitrary"),
    )(*args, cos, sin, ln_g.reshape(1, w), ln_b.reshape(1, w), ws, bias)


def _rope_tables(pos):
    half = MOBA_HEAD_DIM // 2
    inv = jnp.power(ROPE_THETA, -jnp.arange(half, dtype=F32) / half)
    ang = pos.astype(F32)[:, None] * inv[None, :]
    cos = jnp.cos(ang)
    sin = jnp.sin(ang)
    cos_full = jnp.tile(jnp.concatenate([cos, cos], axis=-1), (1, MOBA_HEADS))
    sin_signed = jnp.tile(jnp.concatenate([-sin, sin], axis=-1), (1, MOBA_HEADS))
    return cos_full, sin_signed


def _top_rows(gate, n_sel):
    idx = _iota(gate.shape, 0)
    sel = jnp.zeros(gate.shape, F32)
    cur = gate
    for _ in range(n_sel):
        m = jnp.max(cur, axis=0, keepdims=True)
        first = jnp.min(jnp.where(cur == m, idx, gate.shape[0]), axis=0, keepdims=True)
        one = jnp.logical_and(idx == first, m > NEG_INF)
        sel = jnp.where(one, 1.0, sel)
        cur = jnp.where(one, NEG_INF, cur)
    return sel


def _moba_prompt_kernel(qt_ref, k_ref, vt_ref, km_ref, o_ref, sel_scr, *, group):
    qi = pl.program_id(2)
    n_heads = qt_ref.shape[1]
    d, tq = qt_ref.shape[2], qt_ref.shape[3]
    nb = km_ref.shape[2]
    blk = k_ref.shape[3]

    qs, init = [], []
    for hh in range(n_heads):
        q = qt_ref[0, hh]
        gate = _dot(km_ref[0, hh], q)
        gate = jnp.where(_iota(gate.shape, 0) < qi, gate, NEG_INF)
        sel = _top_rows(gate, min(MOBA_TOPK, nb))
        for r in range(nb):
            sel_scr[hh, r] = sel[r:r + 1, :]
        q = (q.astype(F32) * (MOBA_HEAD_DIM ** -0.5)).astype(BF16)
        qs.append(q)
        s = _dot(k_ref[0, hh, qi], q)
        s = jnp.where(_iota(s.shape, 0) <= _iota(s.shape, 1), s, NEG_INF)
        m0 = jnp.max(s, axis=0, keepdims=True)
        p = jnp.exp(s - m0)
        l0 = jnp.sum(p, axis=0, keepdims=True)
        init.append((m0, l0, _dot(vt_ref[0, hh, qi], p.astype(BF16))))

    def body(jg, carry):
        j0 = pl.multiple_of(jg * group, group)
        units = [(hh, g) for g in range(group) for hh in range(n_heads)]
        state = list(carry)
        score = lambda hh, g: _dot(k_ref[0, hh, j0 + g], qs[hh])
        s_next = score(*units[0])
        for u, (hh, g) in enumerate(units):
            s = s_next
            if u + 1 < len(units):
                s_next = score(*units[u + 1])
            s = jnp.where(sel_scr[hh, j0 + g] > 0.5, s, NEG_INF)
            m_u = jnp.max(s, axis=0, keepdims=True)
            m_ref = jnp.where(m_u > NEG_INF, m_u, 0.0)
            p = jnp.exp(s - m_ref)
            l_u = jnp.sum(p, axis=0, keepdims=True)
            pv = _dot(vt_ref[0, hh, j0 + g], p.astype(BF16))
            m, l, acc = state[hh]
            m_new = jnp.maximum(m, m_u)
            w_old = jnp.exp(m - m_new)
            w_u = jnp.exp(m_u - m_new)
            state[hh] = (m_new, l * w_old + l_u * w_u, acc * w_old + pv * w_u)
        return tuple(state)

    final = lax.fori_loop(0, (qi + group - 1) // group, body, tuple(init))
    for hh in range(n_heads):
        _, l, acc = final[hh]
        o_ref[0, hh] = (acc / l).astype(BF16)


def _moba_prompt(qt_hm, k_blk, vt_blk, km_hm):
    bsz, nh, d, seq = qt_hm.shape
    nb = k_blk.shape[2]
    tq = MOBA_BLOCK
    group = math.gcd(nb, 4)
    hps = 2
    return pl.pallas_call(
        functools.partial(_moba_prompt_kernel, group=group),
        name="moba_prompt",
        out_shape=jax.ShapeDtypeStruct((bsz, nh, d, seq), BF16),
        grid=(bsz, nh // hps, seq // tq),
        in_specs=[pl.BlockSpec((1, hps, d, tq), lambda b, h, i: (b, h, 0, i)),
                  pl.BlockSpec((1, hps, nb, MOBA_BLOCK, d), lambda b, h, i: (b, h, 0, 0, 0)),
                  pl.BlockSpec((1, hps, nb, d, MOBA_BLOCK), lambda b, h, i: (b, h, 0, 0, 0)),
                  pl.BlockSpec((1, hps, nb, d), lambda b, h, i: (b, h, 0, 0))],
        out_specs=pl.BlockSpec((1, hps, d, tq), lambda b, h, i: (b, h, 0, i)),
        scratch_shapes=[pltpu.VMEM((hps, nb, 1, tq), F32)],
        compiler_params=_cparams("arbitrary", "arbitrary", "arbitrary"),
    )(qt_hm, k_blk, vt_blk, km_hm)


def _bdot(a, b):
    return lax.dot_general(a, b, (((2,), (1,)), ((0,), (0,))), preferred_element_type=F32)


def _bdot_nt(a, b):
    return lax.dot_general(a, b, (((2,), (2,)), ((0,), (0,))), preferred_element_type=F32)


def _moba_sample_kernel(pt_ref, q_ref, kn_ref, vn_ref, *rest, n_new, bps):
    page_refs = rest[:4 * bps]
    o_ref, o_scr, m_scr, l_scr, ks_scr = rest[4 * bps:]
    j = pl.program_id(1)
    nblk = pl.num_programs(1)
    q = q_ref[0]
    nh, rows, d = q.shape
    qs = (q.astype(F32) * (MOBA_HEAD_DIM ** -0.5)).astype(BF16)

    ones = jnp.ones((nh, rows, PAGE_SIZE), BF16)
    for blk in range(bps):
        kt = [page_refs[2 * blk + a][0] for a in range(2)]
        vt = [page_refs[2 * bps + 2 * blk + a][0] for a in range(2)]
        s = jnp.concatenate([_bdot(qs, k.astype(BF16)) for k in kt], axis=-1)
        m = jnp.max(s, axis=-1, keepdims=True)
        p = jnp.exp(s - m)
        l = jnp.sum(p, axis=-1, keepdims=True)
        pb = p.astype(BF16)
        o = (_bdot_nt(pb[:, :, :PAGE_SIZE], vt[0].astype(BF16))
             + _bdot_nt(pb[:, :, PAGE_SIZE:], vt[1].astype(BF16)))
        slot = j * bps + blk
        o_scr[slot] = o
        m_scr[slot] = jnp.broadcast_to(m, o.shape)
        l_scr[slot] = jnp.broadcast_to(l, o.shape)
        ksum = None
        for k in kt:
            hi = k.astype(BF16)
            mid = (k - hi.astype(F32)).astype(BF16)
            part = _bdot_nt(ones, hi) + _bdot_nt(ones, mid)
            ksum = part if ksum is None else ksum + part
        ks_scr[slot] = ksum

    @pl.when(j == nblk - 1)
    def _():
        nb = ks_scr.shape[0]
        kmean = (ks_scr[...] * (1.0 / MOBA_BLOCK)).astype(BF16).astype(F32)
        gate = jnp.sum(q.astype(F32)[None] * kmean, axis=-1, keepdims=True)
        sel = _top_rows(jnp.broadcast_to(gate, kmean.shape), min(MOBA_TOPK, nb)) > 0.5

        s_own = _bdot_nt(qs, kn_ref[0])
        causal = _iota(s_own.shape, 2) <= _iota(s_own.shape, 1)
        s_own = jnp.where(jnp.logical_and(causal, _iota(s_own.shape, 2) < n_new), s_own, NEG_INF)
        m_o = jnp.max(s_own, axis=-1, keepdims=True)
        p_o = jnp.exp(s_own - m_o)
        l_o = jnp.sum(p_o, axis=-1, keepdims=True)
        o_o = _bdot(p_o.astype(BF16), vn_ref[0])

        m_all = m_scr[...]
        m_top = jnp.maximum(m_o, jnp.max(jnp.where(sel, m_all, NEG_INF), axis=0))
        w = jnp.where(sel, jnp.exp(m_all - m_top[None]), 0.0)
        w_o = jnp.exp(m_o - m_top)
        l_tot = l_o * w_o + jnp.sum(w * l_scr[...], axis=0)
        o_tot = o_o * w_o + jnp.sum(w * o_scr[...], axis=0)
        o_ref[0] = o_tot / l_tot


def _moba_sample(page_table, q_hm, kn_hm, vn_hm, cache_kt, cache_vt, *, n_new):
    bsz, nh, rows, d = q_hm.shape
    n_pages = page_table.shape[1]
    pages_per_blk = MOBA_BLOCK // PAGE_SIZE
    assert pages_per_blk == 2 and n_pages % pages_per_blk == 0
    nblk = n_pages // pages_per_blk
    bps = math.gcd(nblk, 2)
    pps = bps * pages_per_blk

    def page_spec(a):
        return pl.BlockSpec((1, nh, d, PAGE_SIZE), lambda b, j, pt: (pt[b * n_pages + pps * j + a], 0, 0, 0))

    seq_spec = pl.BlockSpec((1, nh, rows, d), lambda b, j, pt: (b, 0, 0, 0))
    part = pltpu.VMEM((nblk, nh, rows, d), F32)
    page_specs = [page_spec(a) for a in range(pps)]
    grid_spec = pltpu.PrefetchScalarGridSpec(
        num_scalar_prefetch=1,
        grid=(bsz, nblk // bps),
        in_specs=[seq_spec, seq_spec, seq_spec] + page_specs + page_specs,
        out_specs=seq_spec,
        scratch_shapes=[part, part, part, part])
    return pl.pallas_call(
        functools.partial(_moba_sample_kernel, n_new=n_new, bps=bps),
        name="moba_sample",
        out_shape=jax.ShapeDtypeStruct((bsz, nh, rows, d), F32),
        grid_spec=grid_spec,
        compiler_params=_cparams("arbitrary", "arbitrary"),
    )(page_table.reshape(-1), q_hm, kn_hm, vn_hm, *([cache_kt] * pps), *([cache_vt] * pps))


def _ssd_kernel(z_ref, xs_ref, bm_ref, cm_ref, dt_ref, conv0_ref, st0_ref, cw_ref, cb_ref, dtb_ref,
                alog_ref, dskip_ref, ng_ref, y_ref, stout_ref, convout_ref, st_scr, xbuf, yd_scr, *, lv):
    c = pl.program_id(1)
    last = pl.num_programs(1) - 1
    q = SSD_CHUNK
    hp = SSD_INNER
    gw = hp // SSD_GROUPS

    @pl.when(c == 0)
    def _():
        st_scr[...] = st0_ref[0]
        xbuf[0:8, :] = conv0_ref[0]

    def pad(x):
        if lv == q:
            return x
        return jnp.concatenate([x, jnp.zeros((q - lv, x.shape[1]), x.dtype)], axis=0)

    xbuf[8:8 + q, 0:hp] = pad(xs_ref[...])
    xbuf[8:8 + q, hp:hp + SSD_BC] = pad(bm_ref[...])
    xbuf[8:8 + q, hp + SSD_BC:] = pad(cm_ref[...])
    first = 8 - (SSD_CONV - 1)
    conv = cb_ref[...] + xbuf[first:first + q, :] * cw_ref[0:1, :]
    for tap in range(1, SSD_CONV):
        conv = conv + xbuf[first + tap:first + tap + q, :] * cw_ref[tap:tap + 1, :]

    @pl.when(c == last)
    def _():
        convout_ref[0] = xbuf[8 + lv - (SSD_CONV - 1):8 + lv, :]

    xbuf[0:8, :] = xbuf[q:q + 8, :]

    xbc = jax.nn.silu(conv)
    xs = xbc[:, :hp]
    bmat = xbc[:, hp:hp + SSD_BC]
    cmat = xbc[:, hp + SSD_BC:]

    dt = jax.nn.softplus(pad(dt_ref[...]) + dtb_ref[...])
    if lv != q:
        dt = jnp.where(_iota(dt.shape, 0) < lv, dt, 0.0)
    adt = dt * (-jnp.exp(alog_ref[...]))
    tri = (_iota((q, q), 1) <= _iota((q, q), 0)).astype(BF16)
    hi, mid, lo = _split3(adt)
    a_cum = _dot(tri, hi) + _dot(tri, mid) + _dot(tri, lo)
    a_cum_t = a_cum.T

    head_of_col = _iota((LANES, hp), 1) // SSD_HEAD_DIM
    e_cols = (head_of_col == _iota((LANES, hp), 0)).astype(BF16)
    e_time = (_iota((LANES, SSD_HEADS * q), 1) // q == _iota((LANES, SSD_HEADS * q), 0)).astype(BF16)
    dt_full = _dot_exact_rhs01(dt, e_cols)
    a_full = _dot_exact_rhs01(a_cum, e_cols)
    a_col = _dot_exact_rhs01(a_cum, e_time)
    a_last = a_full[q - 1:q, :]

    xdt = xs * dt_full
    xdt_b = xdt.astype(BF16)
    xdt_end = (xdt * jnp.exp(a_last - a_full)).astype(BF16)
    grow = jnp.exp(a_full)
    carry = jnp.exp(a_last)
    causal = _iota((q, q), 1) <= _iota((q, q), 0)
    low_half = _iota((q, LANES), 1) < SSD_HEAD_DIM
    heads_per_group = SSD_HEADS // SSD_GROUPS

    for g in range(SSD_GROUPS):
        cols = slice(g * gw, (g + 1) * gw)
        cc = cmat[:, g * SSD_STATE:(g + 1) * SSD_STATE].astype(BF16)
        bc = bmat[:, g * SSD_STATE:(g + 1) * SSD_STATE]
        cb = _dot_nt(cc, bc.astype(BF16))
        st_g = st_scr[:, cols]
        y_off = _dot(cc, st_g.astype(BF16)) * grow[:, cols]
        st_scr[:, cols] = st_g * carry[:, cols] + _dot(bc.T.astype(BF16), xdt_end[:, cols])
        for pair in range(heads_per_group // 2):
            h0 = g * heads_per_group + 2 * pair
            pcols = slice(h0 * SSD_HEAD_DIM, (h0 + 2) * SSD_HEAD_DIM)
            outs = []
            for h in (h0, h0 + 1):
                seg = a_col[:, h * q:(h + 1) * q] - a_cum_t[h:h + 1, :]
                decay = jnp.exp(jnp.where(causal, seg, NEG_INF))
                outs.append(_dot((cb * decay).astype(BF16), xdt_b[:, pcols]))
            yd_scr[:, pcols] = jnp.where(low_half, outs[0], outs[1]) + y_off[:, pcols.start - g * gw:pcols.stop - g * gw]

    y = yd_scr[...] + dskip_ref[...] * xs
    y = y * jax.nn.silu(pad(z_ref[...]))
    y = y * lax.rsqrt(jnp.mean(y * y, axis=-1, keepdims=True) + EPS) * ng_ref[...]
    y_ref[...] = y[:lv].astype(y_ref.dtype)

    @pl.when(c == last)
    def _():
        stout_ref[0] = st_scr[...]


def _ssd_scan(proj, conv0p, st0_t, conv_w, conv_b, dt_bias, a_log, dskip_full, norm_g, *, bsz, seq_len):
    lv = min(seq_len, SSD_CHUNK)
    nc = seq_len // lv
    hp = SSD_INNER
    t = proj.shape[0]
    blk = lambda w, cidx: pl.BlockSpec((lv, w), lambda b, c: (b * nc + c, cidx))
    full = lambda shape: pl.BlockSpec(shape, lambda b, c: (0,) * len(shape))
    per_seq = lambda shape: pl.BlockSpec((1,) + shape, lambda b, c: (b,) + (0,) * len(shape))
    return pl.pallas_call(
        functools.partial(_ssd_kernel, lv=lv),
        name="ssd_scan",
        out_shape=[jax.ShapeDtypeStruct((t, hp), BF16 if lv % 16 == 0 else F32),
                   jax.ShapeDtypeStruct((bsz, SSD_STATE, hp), F32),
                   jax.ShapeDtypeStruct((bsz, SSD_CONV - 1, SSD_CONV_DIM), F32)],
        grid=(bsz, nc),
        in_specs=[blk(hp, 0), blk(hp, 1), blk(SSD_BC, 2 * hp // SSD_BC), blk(SSD_BC, 2 * hp // SSD_BC + 1),
                  blk(LANES, (2 * hp + 2 * SSD_BC) // LANES),
                  per_seq((8, SSD_CONV_DIM)), per_seq((SSD_STATE, hp)),
                  full((SSD_CONV, SSD_CONV_DIM)), full((1, SSD_CONV_DIM)), full((1, LANES)), full((1, LANES)),
                  full((1, hp)), full((1, hp))],
        out_specs=[pl.BlockSpec((lv, hp), lambda b, c: (b * nc + c, 0)),
                   per_seq((SSD_STATE, hp)), per_seq((SSD_CONV - 1, SSD_CONV_DIM))],
        scratch_shapes=[pltpu.VMEM((SSD_STATE, hp), F32), pltpu.VMEM((SSD_CHUNK + 8, SSD_CONV_DIM), F32),
                        pltpu.VMEM((SSD_CHUNK, hp), F32)],
        compiler_params=_cparams("arbitrary", "arbitrary"),
    )(proj, proj, proj, proj, proj, conv0p, st0_t, conv_w, conv_b.reshape(1, -1), dt_bias, a_log,
      dskip_full, norm_g.reshape(1, -1))


def _argmax_rows(x):
    rows, n = x.shape
    idx = _iota((rows, n), 0)
    vs = [x[r:r + 8] for r in range(0, rows, 8)]
    ids = [idx[r:r + 8] for r in range(0, rows, 8)]
    while len(vs) > 1:
        nv, ni = [], []
        for p in range(0, len(vs) - 1, 2):
            nv.append(jnp.maximum(vs[p], vs[p + 1]))
            ni.append(jnp.where(vs[p] >= vs[p + 1], ids[p], ids[p + 1]))
        if len(vs) % 2:
            nv.append(vs[-1])
            ni.append(ids[-1])
        vs, ids = nv, ni
    m = jnp.max(vs[0], axis=0, keepdims=True)
    first = jnp.min(jnp.where(vs[0] == m, ids[0], rows), axis=0, keepdims=True)
    return m, first


def _extract_top(cur_scr, n_chain, k):
    rows, n = cur_scr.shape[1], cur_scr.shape[2]
    idx = _iota((rows, n), 0)
    slot = _iota((k, n), 0)

    def body(i, carry):
        out = []
        for c in range(n_chain):
            vals, picks = carry[c]
            cur = cur_scr[c]
            m, first = _argmax_rows(cur)
            cur_scr[c] = jnp.where(idx == first, NEG_INF, cur)
            out.append((jnp.where(slot == i, m, vals), jnp.where(slot == i, first, picks)))
        return tuple(out)

    init = tuple((jnp.zeros((k, n), F32), jnp.zeros((k, n), jnp.int32)) for _ in range(n_chain))
    return lax.fori_loop(0, k, body, init)


def _candidate_pieces(k):
    assert k == 16
    pieces = [(0, 1, 16, 16), (1, 2, 8, 8)]
    pieces += [(a, a + 1, 8, k // (a + 1)) for a in range(2, 8)]
    pieces.append((8, 16, 1, 1))
    return pieces


def _peer_route_kernel(qt_ref, sk_ref, alpha_ref, count_ref, beta_ref, rank_ref, cur_scr, cand_scr):
    k = PEER_TOPK
    hd = sk_ref.shape[3]
    nk = sk_ref.shape[2]
    pieces = _candidate_pieces(k)
    n_lt = qt_ref.shape[1] // LANES
    idx = _iota((nk, LANES), 0)

    scores = []
    for lt in range(n_lt):
        lanes = slice(lt * LANES, (lt + 1) * LANES)
        for c in range(2):
            s = _dot(sk_ref[0, c], qt_ref[c * hd:(c + 1) * hd, lanes])
            cur_scr[2 * lt + c] = s
            scores.append(s)
    tops = _extract_top(cur_scr, 2 * n_lt, k)

    for lt in range(n_lt):
        v1, v2 = tops[2 * lt][0], tops[2 * lt + 1][0]
        cands = []
        for a_lo, a_hi, n_b, n_valid in pieces:
            if a_hi - a_lo == 1:
                c = v1[a_lo:a_hi, :] + v2[0:n_b, :]
                cands.append(jnp.where(_iota(c.shape, 0) < n_valid, c, NEG_INF))
            else:
                cands.append(v1[a_lo:a_hi, :] + v2[0:1, :])
        cand_scr[lt] = jnp.concatenate(cands, axis=0)
    picked = _extract_top(cand_scr, n_lt, k)

    for lt in range(n_lt):
        lanes = slice(lt * LANES, (lt + 1) * LANES)
        (v1, rows1), (v2, rows2) = tops[2 * lt], tops[2 * lt + 1]
        best, best_rows = picked[lt]
        z = jnp.sum(jnp.exp(best - best[0:1, :]), axis=0, keepdims=True)
        count = jnp.zeros((nk, LANES), F32)
        rank2 = jnp.full((nk, LANES), float(k), F32)
        row = 0
        for a_lo, a_hi, n_b, _ in pieces:
            for a in range(a_lo, a_hi):
                in_piece = jnp.logical_and(best_rows >= row, best_rows < row + n_b)
                n_a = jnp.sum(jnp.where(in_piece, 1.0, 0.0), axis=0, keepdims=True)
                count = jnp.where(idx == rows1[a:a + 1, :], n_a, count)
                row += n_b
        for b in range(k):
            rank2 = jnp.where(idx == rows2[b:b + 1, :], float(b), rank2)
        alpha_ref[0, :, lanes] = jnp.exp(scores[2 * lt] - v1[0:1, :]) / z
        count_ref[0, :, lanes] = count
        beta_ref[0, :, lanes] = jnp.exp(scores[2 * lt + 1] - v2[0:1, :])
        rank_ref[0, :, lanes] = rank2


def _peer_route(q_t, subkeys, *, tt):
    t = q_t.shape[1]
    nh, _, nk, hd = subkeys.shape
    out = jax.ShapeDtypeStruct((nh, nk, t), F32)
    ospec = pl.BlockSpec((1, nk, tt), lambda i, h: (h, 0, i))
    n_cand = sum((a_hi - a_lo) * n_b for a_lo, a_hi, n_b, _ in _candidate_pieces(PEER_TOPK))
    return pl.pallas_call(
        _peer_route_kernel,
        name="peer_route",
        out_shape=[out, out, out, out],
        grid=(t // tt, nh),
        in_specs=[pl.BlockSpec((2 * hd, tt), lambda i, h: (h, i)),
                  pl.BlockSpec((1, 2, nk, hd), lambda i, h: (h, 0, 0, 0))],
        out_specs=[ospec, ospec, ospec, ospec],
        scratch_shapes=[pltpu.VMEM((2 * (tt // LANES), nk, LANES), F32),
                        pltpu.VMEM((tt // LANES, n_cand, LANES), F32)],
        compiler_params=_cparams("arbitrary", "arbitrary"),
    )(q_t, subkeys)


def _peer_dense_kernel(ht_ref, u_ref, vt_ref, alpha_ref, count_ref, beta_ref, rank_ref, o_ref, aw_scr):
    e = pl.program_id(1)
    nh, nk, tt = beta_ref.shape
    n_sub, _, sub = vt_ref.shape
    per_sub = sub // nk
    ib = n_sub * per_sub

    @pl.when(e == 0)
    def _():
        o_ref[...] = jnp.zeros(o_ref.shape, F32)

    i0 = pl.multiple_of(e * ib, ib)
    s_next = _dot(u_ref[0:sub, :], ht_ref[...])
    acc = None
    for c in range(n_sub):
        s_cur = s_next
        if c + 1 < n_sub:
            s_next = _dot(u_ref[(c + 1) * sub:(c + 2) * sub, :], ht_ref[...])
        for lt in range(tt // LANES):
            lanes = slice(lt * LANES, (lt + 1) * LANES)
            counts = [count_ref[h, pl.ds(i0, ib), lanes] for h in range(nh)]
            alphas = [alpha_ref[h, pl.ds(i0, ib), lanes] for h in range(nh)]
            for kk in range(per_sub):
                ii = c * per_sub + kk
                rows = slice(kk * nk, (kk + 1) * nk)
                w = jnp.zeros((nk, LANES), F32)
                for h in range(nh):
                    hit = rank_ref[h, :, lanes] < counts[h][ii:ii + 1, :]
                    w = w + jnp.where(hit, beta_ref[h, :, lanes], 0.0) * alphas[h][ii:ii + 1, :]
                aw_scr[c, rows, lanes] = (w * _gelu(s_cur[rows, lanes])).astype(BF16)
        part = _dot(vt_ref[c], aw_scr[c])
        acc = part if acc is None else acc + part
    o_ref[...] += acc


def _peer_dense(h_t, u_tabs, vt_tiles, layer, route, *, tt, tile_subs):
    d, t = h_t.shape
    alpha, count, beta, rank = route
    nh, nk, _ = alpha.shape
    sub = vt_tiles.shape[3]
    te = tile_subs * sub
    rspec = pl.BlockSpec((nh, nk, tt), lambda i, e: (0, 0, i))
    return pl.pallas_call(
        _peer_dense_kernel,
        name="peer_dense",
        out_shape=jax.ShapeDtypeStruct((d, t), F32),
        grid=(t // tt, nk * nk // te),
        in_specs=[pl.BlockSpec((d, tt), lambda i, e: (0, i)),
                  pl.BlockSpec((None, te, d), lambda i, e: (layer, e, 0)),
                  pl.BlockSpec((None, tile_subs, d, sub), lambda i, e: (layer, e, 0, 0)),
                  rspec, rspec, rspec, rspec],
        out_specs=pl.BlockSpec((d, tt), lambda i, e: (0, i)),
        scratch_shapes=[pltpu.VMEM((tile_subs, sub, tt), BF16)],
        compiler_params=_cparams("arbitrary", "arbitrary"),
    )(h_t, u_tabs, vt_tiles, alpha, count, beta, rank)


def _prep_weights(norm_mix_g, mod_mix_w, mod_mix_b, norm_ffn_g, mod_ffn_w, mod_ffn_b,
                  w_in_ab, w_out_ab, gmlp_ln_g, gmlp_ln_b, gmlp_ws, gmlp_bs,
                  ssd_w_in, ssd_conv_w, ssd_conv_b, ssd_dt_bias, ssd_a_log, ssd_d, ssd_norm_g, ssd_w_out,
                  peer_wq, peer_subkeys, peer_u, peer_v, final_norm_g):
    depth = norm_mix_g.shape[0]
    d = norm_mix_g.shape[1]
    p = dict(depth=depth, norm_mix_g=norm_mix_g, norm_ffn_g=norm_ffn_g, final_norm_g=final_norm_g,
             gmlp_ln_g=gmlp_ln_g, gmlp_ln_b=gmlp_ln_b, gmlp_ws=gmlp_ws, gmlp_bs=gmlp_bs,
             ssd_conv_w=ssd_conv_w, ssd_conv_b=ssd_conv_b, ssd_norm_g=ssd_norm_g)
    mod_w, mod_b = [], []
    for i in range(depth):
        mod_w += [mod_mix_w[i], mod_ffn_w[i]]
        mod_b += [mod_mix_b[i], mod_ffn_b[i]]
    p["mod_w"] = jnp.stack(mod_w).astype(BF16)
    p["mod_b"] = jnp.stack(mod_b)[:, None, :]
    p["w_in_ab"] = w_in_ab.astype(BF16)
    p["w_out_ab"] = w_out_ab.astype(BF16)
    n_ssd = ssd_w_in.shape[0]
    z_w = ssd_w_in[:, :, :SSD_INNER]
    xbc_w = ssd_w_in[:, :, SSD_INNER:SSD_INNER + SSD_CONV_DIM]
    dt_w = ssd_w_in[:, :, SSD_INNER + SSD_CONV_DIM:]
    used = 2 * SSD_INNER + 2 * SSD_BC + LANES
    total = -(-used // 1792) * 1792
    p["ssd_w_in"] = jnp.concatenate(
        [z_w, xbc_w, dt_w, jnp.zeros((n_ssd, d, total - used + LANES - SSD_HEADS), F32)], axis=-1).astype(BF16)
    pad_heads = lambda a, v: jnp.concatenate([a, jnp.full((n_ssd, LANES - SSD_HEADS), v, F32)], axis=-1)[:, None, :]
    p["ssd_dt_bias"] = pad_heads(ssd_dt_bias, 0.0)
    p["ssd_a_log"] = pad_heads(ssd_a_log, 0.0)
    p["ssd_dskip"] = jnp.repeat(ssd_d, SSD_HEAD_DIM, axis=-1)[:, None, :]
    p["ssd_w_out"] = ssd_w_out.astype(BF16)
    p["peer_wq"] = peer_wq.astype(BF16)
    p["peer_subkeys"] = peer_subkeys.astype(BF16)
    p["peer_u"] = peer_u.astype(BF16)
    n_keys = peer_subkeys.shape[3]
    sub = PEER_FIRST_KEYS_PER_SUBTILE * n_keys
    p["peer_vt"] = peer_v.astype(BF16).reshape(depth, n_keys * n_keys // sub, sub, d).transpose(0, 1, 3, 2)
    return p


def _peer_layer(x, mod, seq_len, p, i, *, tm, final):
    q_t, h_t = _adaln_matmul(x, p["norm_ffn_g"][i], mod, seq_len, p["peer_wq"][i], tm=tm,
                             tn=p["peer_wq"].shape[2] // 2, out_dtype=BF16, feature_major=True)
    nk = p["peer_subkeys"].shape[3]
    route = _peer_route(q_t, p["peer_subkeys"][i], tt=min(256, tm))
    y_t = _peer_dense(h_t, p["peer_u"], p["peer_vt"], i, route, tt=min(256, tm),
                      tile_subs=PEER_SUBTILES_PER_STEP)
    return _residual_t(x, y_t, mod, seq_len, tm=tm, final_g=p["final_norm_g"] if final else None)


def _ab_layer(x, mod, bsz, seq_len, pos, past, p, i, j, *, tm):
    t, d = x.shape
    w = GMLP_GROUPS * (d // 16)
    proj = _adaln_matmul(x, p["norm_mix_g"][i], mod, seq_len, p["w_in_ab"][j], tm=tm,
                         tn=p["w_in_ab"].shape[2] // 2)
    cos, sin = _rope_tables(pos)
    lc = min(seq_len, GMLP_CHUNK)
    ws = p["gmlp_ws"][j][:, :lc, :lc]
    bias = jnp.repeat(p["gmlp_bs"][j][:, :lc].T, w // GMLP_GROUPS, axis=1)
    if seq_len < GMLP_CHUNK:
        reps = tm // seq_len
        ws = jnp.where(jnp.tril(jnp.ones((lc, lc), bool)), ws, 0.0)
        eye = jnp.eye(reps, dtype=F32)
        ws = jnp.einsum("ab,gts->gatbs", eye, ws).reshape(GMLP_GROUPS, tm, tm)
        bias = jnp.tile(bias, (reps, 1))
        cos = jnp.tile(cos, (reps, 1))
        sin = jnp.tile(sin, (reps, 1))
        tr, lc_eff = tm, tm
    else:
        tr, lc_eff = MOBA_BLOCK, lc
    post = _ab_post(proj, cos, sin, p["gmlp_ln_g"][j], p["gmlp_ln_b"][j], ws.astype(BF16), bias,
                    tr=tr, lc=lc_eff, head_major=past is None, bsz=bsz)
    a_out, vln, k_rot, kmean = post[:4]
    v_new = proj[:, 2 * w + 2 * w:2 * w + 3 * w]
    nh, hd = MOBA_HEADS, MOBA_HEAD_DIM
    if past is None:
        nb = seq_len // MOBA_BLOCK
        qt_hm, k_blk, vt_blk = post[4:]
        km_hm = kmean.reshape(bsz, nb, nh, hd).transpose(0, 2, 1, 3).astype(BF16)
        o_t = _moba_prompt(qt_hm, k_blk, vt_blk, km_hm)
        b_out = o_t.transpose(0, 3, 1, 2).reshape(t, w)
    else:
        q_rot = post[4]
        page_table, cache_k, cache_v = past
        rows = -(-seq_len // 16) * 16

        def head_major(a):
            a = a.astype(BF16).reshape(bsz, seq_len, nh, hd).transpose(0, 2, 1, 3)
            return jnp.pad(a, ((0, 0), (0, 0), (0, rows - seq_len), (0, 0)))

        b_out = _moba_sample(page_table, head_major(q_rot), head_major(k_rot), head_major(v_new),
                             cache_k.transpose(0, 2, 3, 1), cache_v.transpose(0, 2, 3, 1), n_new=seq_len)
        b_out = b_out[:, :, :seq_len].transpose(0, 2, 1, 3).reshape(t, w).astype(BF16)
    x = _proj_residual([a_out, b_out], p["w_out_ab"][j], x, mod, seq_len, tm=tm)
    lcv = min(seq_len, GMLP_CHUNK)
    gv = vln.reshape(bsz, seq_len, w)[:, seq_len - lcv:]
    return (x, k_rot.reshape(bsz, seq_len, nh, hd), v_new.reshape(bsz, seq_len, nh, hd), gv)


def _ssd_layer(x, mod, bsz, seq_len, ssm0, conv0, p, i, j, *, tm):
    t, d = x.shape
    proj = _adaln_matmul(x, p["norm_mix_g"][i], mod, seq_len, p["ssd_w_in"][j], tm=tm, tn=1792)
    if conv0 is None:
        conv0 = jnp.zeros((bsz, SSD_CONV - 1, SSD_CONV_DIM), F32)
    if ssm0 is None:
        st0_t = jnp.zeros((bsz, SSD_STATE, SSD_INNER), F32)
    else:
        st0_t = ssm0.reshape(bsz, SSD_INNER, SSD_STATE).transpose(0, 2, 1)
    conv0p = jnp.pad(conv0, ((0, 0), (8 - (SSD_CONV - 1), 0), (0, 0)))
    y, st_t, conv_new = _ssd_scan(proj, conv0p, st0_t, p["ssd_conv_w"][j], p["ssd_conv_b"][j],
                                  p["ssd_dt_bias"][j], p["ssd_a_log"][j], p["ssd_dskip"][j],
                                  p["ssd_norm_g"][j], bsz=bsz, seq_len=seq_len)
    x = _proj_residual([y], p["ssd_w_out"][j], x, mod, seq_len, tm=tm)
    ssm_new = st_t.transpose(0, 2, 1).reshape(bsz, SSD_HEADS, SSD_HEAD_DIM, SSD_STATE)
    return x, ssm_new, conv_new


def _run_trunk(x, c, pos0, past_kv, ssm_past, conv_past, p):
    bsz, seq_len, d = x.shape
    t = bsz * seq_len
    tm = next(c for c in (1024, 512, 256) if (t % c == 0 and seq_len % c == 0) or c == 256)
    assert t % tm == 0
    x = x.reshape(t, d)
    pos = pos0 + jnp.arange(seq_len, dtype=jnp.int32)
    mods = _mod_all(c, p["mod_w"], p["mod_b"])
    depth = p["depth"]
    k_rows, v_rows, gv_rows, ssm_new, conv_new = [], [], [], [], []
    for i in range(depth):
        j = i // 2
        if i % 2 == 0:
            past = None if past_kv is None else (past_kv[0], past_kv[1][j], past_kv[2][j])
            x, k_new, v_new, gv = _ab_layer(x, mods[2 * i], bsz, seq_len, pos, past, p, i, j, tm=tm)
            k_rows.append(k_new)
            v_rows.append(v_new)
            gv_rows.append(gv)
        else:
            x, s_new, cv_new = _ssd_layer(x, mods[2 * i], bsz, seq_len,
                                          None if ssm_past is None else ssm_past[j],
                                          None if conv_past is None else conv_past[j], p, i, j, tm=tm)
            ssm_new.append(s_new)
            conv_new.append(cv_new)
        x = _peer_layer(x, mods[2 * i + 1], seq_len, p, i, tm=tm, final=(i == depth - 1))
    y = x.reshape(bsz, seq_len, d)
    return y, jnp.stack(k_rows), jnp.stack(v_rows), jnp.stack(gv_rows), jnp.stack(ssm_new), jnp.stack(conv_new)


def kernel(x_prompt, x_sample, cache_k, cache_v, page_table, state_ssm, state_conv, c_prompt, c_sample,
           norm_mix_g, mod_mix_w, mod_mix_b, norm_ffn_g, mod_ffn_w, mod_ffn_b,
           w_in_ab, w_out_ab, gmlp_ln_g, gmlp_ln_b, gmlp_ws, gmlp_bs,
           ssd_w_in, ssd_conv_w, ssd_conv_b, ssd_dt_bias, ssd_a_log, ssd_d, ssd_norm_g, ssd_w_out,
           peer_wq, peer_subkeys, peer_u, peer_v, final_norm_g):
    p = _prep_weights(norm_mix_g, mod_mix_w, mod_mix_b, norm_ffn_g, mod_ffn_w, mod_ffn_b,
                      w_in_ab, w_out_ab, gmlp_ln_g, gmlp_ln_b, gmlp_ws, gmlp_bs,
                      ssd_w_in, ssd_conv_w, ssd_conv_b, ssd_dt_bias, ssd_a_log, ssd_d, ssd_norm_g, ssd_w_out,
                      peer_wq, peer_subkeys, peer_u, peer_v, final_norm_g)
    prompt = _run_trunk(x_prompt, c_prompt, 0, None, None, None, p)
    past_len = page_table.shape[1] * PAGE_SIZE
    sample = _run_trunk(x_sample, c_sample, past_len, (page_table, cache_k, cache_v),
                        state_ssm, state_conv, p)
    return (prompt[0], sample[0]) + prompt[1:] + sample[1:]
```

```python
import functools
import math

import jax
import jax.numpy as jnp
from jax import lax
from jax.experimental import pallas as pl
from jax.experimental.pallas import tpu as pltpu

F32 = jnp.float32
BF16 = jnp.bfloat16
NEG_INF = float("-inf")

EPS = 1e-6
LANES = 128
GMLP_GROUPS = 8
GMLP_CHUNK = 128
MOBA_HEADS = 8
MOBA_HEAD_DIM = 64
MOBA_BLOCK = 256
MOBA_TOPK = 3
PAGE_SIZE = 128
ROPE_THETA = 10000.0
SSD_HEADS = 32
SSD_HEAD_DIM = 64
SSD_GROUPS = 4
SSD_STATE = 128
SSD_CONV = 4
SSD_CHUNK = 128
SSD_INNER = SSD_HEADS * SSD_HEAD_DIM
SSD_BC = SSD_GROUPS * SSD_STATE
SSD_CONV_DIM = SSD_INNER + 2 * SSD_BC
PEER_HEADS = 8
PEER_TOPK = 16
VMEM_LIMIT_BYTES = 56 * 1024 * 1024

TOKEN_TILES = (1024, 512, 256)
SSD_IN_COL_TILE = 1792
PEER_TOKEN_TILE = 256
PEER_FIRST_KEYS_PER_SUBTILE = 2
PEER_SUBTILES_PER_STEP = 4
MOBA_HEADS_PER_STEP = 2
MOBA_BLOCKS_PER_TRIP = 4
MOBA_SAMPLE_BLOCKS_PER_STEP = 4


def _token_tile(t, seq_len):
    for c in TOKEN_TILES:
        if t % c == 0 and seq_len % c == 0:
            return c
    return TOKEN_TILES[-1]


def _cparams(*sem):
    return pltpu.CompilerParams(dimension_semantics=sem, vmem_limit_bytes=VMEM_LIMIT_BYTES)


def _dot(a, b):
    return jnp.dot(a, b, preferred_element_type=F32)


def _dot_nt(a, b):
    return lax.dot_general(a, b, (((1,), (1,)), ((), ())), preferred_element_type=F32)


def _iota(shape, dim):
    return lax.broadcasted_iota(jnp.int32, shape, dim)


def _gelu(x):
    return 0.5 * x * (1.0 + lax.erf(x * (2.0 ** -0.5)))


def _split3(x):
    hi = x.astype(BF16)
    r = x - hi.astype(F32)
    mid = r.astype(BF16)
    lo = (r - mid.astype(F32)).astype(BF16)
    return hi, mid, lo


def _dot_split(parts, e):
    hi, mid, lo = parts
    return _dot(hi, e) + _dot(mid, e) + _dot(lo, e)


def _mod_kernel(c_ref, w_ref, b_ref, o_ref):
    o_ref[0] = _dot(c_ref[...].astype(BF16), w_ref[0]) + b_ref[0]


def _mod_all(c, w4, b4):
    bsz, d = c.shape
    rows = -(-bsz // 16) * 16
    cp = jnp.pad(c, ((0, rows - bsz), (0, 0)))
    n = w4.shape[0]
    out = pl.pallas_call(
        _mod_kernel,
        name="adaln_mod",
        out_shape=jax.ShapeDtypeStruct((n, rows, 3 * d), F32),
        grid=(n, 3),
        in_specs=[pl.BlockSpec((rows, d), lambda i, j: (0, 0)),
                  pl.BlockSpec((1, d, d), lambda i, j: (i, 0, j)),
                  pl.BlockSpec((1, 1, d), lambda i, j: (i, 0, j))],
        out_specs=pl.BlockSpec((1, rows, d), lambda i, j: (i, 0, j)),
        compiler_params=_cparams("arbitrary", "arbitrary"),
    )(cp, w4, b4)
    return out[:, :bsz]


def _mod_rows(mod, seq_len, tm):
    bsz, w = mod.shape
    if seq_len % tm == 0:
        return mod[:, None, :], seq_len // tm
    assert tm % seq_len == 0 and (bsz * seq_len) % tm == 0
    return jnp.repeat(mod, seq_len, axis=0).reshape(bsz * seq_len // tm, tm, w), 1


def _adaln_mm_kernel(x_ref, g_ref, sh_ref, sc_ref, w_ref, o_ref, *rest, feature_major):
    if feature_major:
        ht_ref, h_scr = rest
    else:
        (h_scr,) = rest

    @pl.when(pl.program_id(1) == 0)
    def _():
        x = x_ref[...]
        y = x * lax.rsqrt(jnp.mean(x * x, axis=-1, keepdims=True) + EPS)
        y = y * g_ref[...]
        h = y * (1.0 + sc_ref[0]) + sh_ref[0]
        h_scr[...] = h.astype(BF16)
        if feature_major:
            ht_ref[...] = h.T.astype(BF16)

    out = _dot(h_scr[...], w_ref[...])
    o_ref[...] = (out.T if feature_major else out).astype(o_ref.dtype)


def _adaln_matmul(x, g, mod, seq_len, w, *, tm, tn, out_dtype=F32, feature_major=False):
    t, d = x.shape
    n = w.shape[1]
    assert t % tm == 0 and n % tn == 0
    mod3, tpg = _mod_rows(mod, seq_len, tm)
    r = mod3.shape[1]
    if feature_major:
        out_shape = [jax.ShapeDtypeStruct((n, t), out_dtype), jax.ShapeDtypeStruct((d, t), BF16)]
        out_specs = [pl.BlockSpec((tn, tm), lambda i, j: (j, i)), pl.BlockSpec((d, tm), lambda i, j: (0, i))]
    else:
        out_shape = [jax.ShapeDtypeStruct((t, n), out_dtype)]
        out_specs = [pl.BlockSpec((tm, tn), lambda i, j: (i, j))]
    res = pl.pallas_call(
        functools.partial(_adaln_mm_kernel, feature_major=feature_major),
        name="adaln_matmul",
        out_shape=out_shape,
        grid=(t // tm, n // tn),
        in_specs=[pl.BlockSpec((tm, d), lambda i, j: (i, 0)),
                  pl.BlockSpec((1, d), lambda i, j: (0, 0)),
                  pl.BlockSpec((1, r, d), lambda i, j: (i // tpg, 0, 0)),
                  pl.BlockSpec((1, r, d), lambda i, j: (i // tpg, 0, 1)),
                  pl.BlockSpec((d, tn), lambda i, j: (0, j))],
        out_specs=out_specs,
        scratch_shapes=[pltpu.VMEM((tm, d), BF16)],
        compiler_params=_cparams("arbitrary", "arbitrary"),
    )(x, g.reshape(1, d), mod3, mod3, w)
    return res if feature_major else res[0]


def _proj_res_kernel(*refs, ks):
    lhs = refs[:len(ks)]
    w_ref, x_ref, gate_ref, o_ref = refs[len(ks):]
    acc = None
    off = 0
    for l_ref, k in zip(lhs, ks):
        part = _dot(l_ref[...].astype(BF16), w_ref[off:off + k, :])
        acc = part if acc is None else acc + part
        off += k
    o_ref[...] = x_ref[...] + gate_ref[0] * acc


def _proj_residual(lhs, w, x, mod, seq_len, *, tm):
    t, d = x.shape
    ks = tuple(l.shape[1] for l in lhs)
    mod3, tpg = _mod_rows(mod, seq_len, tm)
    r = mod3.shape[1]
    in_specs = [pl.BlockSpec((tm, k), lambda i: (i, 0)) for k in ks]
    in_specs += [pl.BlockSpec((sum(ks), d), lambda i: (0, 0)),
                 pl.BlockSpec((tm, d), lambda i: (i, 0)),
                 pl.BlockSpec((1, r, d), lambda i: (i // tpg, 0, 2))]
    return pl.pallas_call(
        functools.partial(_proj_res_kernel, ks=ks),
        name="proj_residual",
        out_shape=jax.ShapeDtypeStruct((t, d), F32),
        grid=(t // tm,),
        in_specs=in_specs,
        out_specs=pl.BlockSpec((tm, d), lambda i: (i, 0)),
        compiler_params=_cparams("arbitrary"),
    )(*lhs, w, x, mod3)


def _res_kernel(x_ref, yt_ref, gate_ref, *rest, final_norm):
    x2 = x_ref[...] + gate_ref[0] * yt_ref[...].T
    if final_norm:
        g_ref, o_ref = rest
        o_ref[...] = x2 * lax.rsqrt(jnp.mean(x2 * x2, axis=-1, keepdims=True) + EPS) * g_ref[...]
    else:
        (o_ref,) = rest
        o_ref[...] = x2


def _residual_t(x, y_t, mod, seq_len, *, tm, final_g=None):
    t, d = x.shape
    mod3, tpg = _mod_rows(mod, seq_len, tm)
    r = mod3.shape[1]
    in_specs = [pl.BlockSpec((tm, d), lambda i: (i, 0)),
                pl.BlockSpec((d, tm), lambda i: (0, i)),
                pl.BlockSpec((1, r, d), lambda i: (i // tpg, 0, 2))]
    args = [x, y_t, mod3]
    if final_g is not None:
        in_specs.append(pl.BlockSpec((1, d), lambda i: (0, 0)))
        args.append(final_g.reshape(1, d))
    return pl.pallas_call(
        functools.partial(_res_kernel, final_norm=final_g is not None),
        name="peer_residual",
        out_shape=jax.ShapeDtypeStruct((t, d), F32),
        grid=(t // tm,),
        in_specs=in_specs,
        out_specs=pl.BlockSpec((tm, d), lambda i: (i, 0)),
        compiler_params=_cparams("arbitrary"),
    )(*args)


def _rope(x, cos, sin_signed):
    lane = _iota(x.shape, 1) % MOBA_HEAD_DIM
    half = MOBA_HEAD_DIM // 2
    width = x.shape[1]
    partner = jnp.where(lane < half, pltpu.roll(x, width - half, 1), pltpu.roll(x, half, 1))
    return x * cos + partner * sin_signed


def _ab_post_kernel(u_ref, v_ref, q_ref, k_ref, *rest, lc, n_chunks, head_major):
    if head_major:
        (vb_ref, cos_ref, sin_ref, lng_ref, lnb_ref, ws_ref, bias_ref,
         a_ref, vln_ref, kr_ref, km_ref, qt_ref, khm_ref, vt_ref) = rest
    else:
        (cos_ref, sin_ref, lng_ref, lnb_ref, ws_ref, bias_ref,
         a_ref, vln_ref, kr_ref, km_ref, qr_ref) = rest
    u = _gelu(u_ref[...])
    v = _gelu(v_ref[...])
    mu = jnp.mean(v, axis=-1, keepdims=True)
    var = jnp.mean(jnp.square(v - mu), axis=-1, keepdims=True)
    vln = (v - mu) * lax.rsqrt(var + EPS) * lng_ref[...] + lnb_ref[...]
    vln_ref[...] = vln

    width = vln.shape[1]
    lane_group = _iota((lc, width), 1) // (width // GMLP_GROUPS)
    causal = _iota((lc, lc), 1) <= _iota((lc, lc), 0)
    w_tril = [jnp.where(causal, ws_ref[g], jnp.zeros((), BF16)) for g in range(GMLP_GROUPS)]
    for c in range(n_chunks):
        rows = slice(c * lc, (c + 1) * lc)
        vc = vln[rows].astype(BF16)
        mix = jnp.zeros((lc, width), F32)
        for g in range(GMLP_GROUPS):
            mix = mix + jnp.where(lane_group == g, _dot(w_tril[g], vc), 0.0)
        a_ref[rows, :] = (u[rows] * (mix + bias_ref[...])).astype(BF16)

    cos = cos_ref[...]
    sin = sin_ref[...]
    qr = _rope(q_ref[...], cos, sin)
    kr = _rope(k_ref[...], cos, sin)
    kr_ref[...] = kr
    km_ref[0] = jnp.mean(kr, axis=0, keepdims=True)
    if head_major:
        tr = qr.shape[0]
        qt_ref[0] = qr.T.reshape(MOBA_HEADS, MOBA_HEAD_DIM, tr).astype(BF16)
        vt_ref[0, :, 0] = vb_ref[...].T.reshape(MOBA_HEADS, MOBA_HEAD_DIM, tr).astype(BF16)
        for h in range(MOBA_HEADS):
            khm_ref[0, h, 0] = kr[:, h * MOBA_HEAD_DIM:(h + 1) * MOBA_HEAD_DIM].astype(BF16)
    else:
        qr_ref[...] = qr.astype(BF16)


def _ab_post(proj, cos, sin, ln_g, ln_b, ws, bias, *, tr, lc, head_major, bsz):
    t = proj.shape[0]
    w = ln_g.shape[0]
    n_tab = cos.shape[0] // tr
    n_tiles = t // tr
    nh, hd = MOBA_HEADS, MOBA_HEAD_DIM
    col = lambda c: pl.BlockSpec((tr, w), lambda i: (i, c))
    tab = pl.BlockSpec((tr, w), lambda i: (i % n_tab, 0))
    row = pl.BlockSpec((tr, w), lambda i: (i, 0))
    params = [pl.BlockSpec((1, w), lambda i: (0, 0)),
              pl.BlockSpec((1, w), lambda i: (0, 0)),
              pl.BlockSpec((GMLP_GROUPS, lc, lc), lambda i: (0, 0, 0)),
              pl.BlockSpec((lc, w), lambda i: (0, 0))]
    out_shape = [jax.ShapeDtypeStruct((t, w), BF16),
                 jax.ShapeDtypeStruct((t, w), F32),
                 jax.ShapeDtypeStruct((t, w), F32),
                 jax.ShapeDtypeStruct((n_tiles, 1, w), F32)]
    out_specs = [row, row, row, pl.BlockSpec((1, 1, w), lambda i: (i, 0, 0))]
    if head_major:
        assert tr == MOBA_BLOCK
        in_specs = [col(0), col(1), col(2), col(3), col(4), tab, tab] + params
        args = (proj,) * 5
        out_shape += [jax.ShapeDtypeStruct((bsz, nh, hd, n_tab * tr), BF16),
                      jax.ShapeDtypeStruct((bsz, nh, n_tab, tr, hd), BF16),
                      jax.ShapeDtypeStruct((bsz, nh, n_tab, hd, tr), BF16)]
        out_specs += [pl.BlockSpec((1, nh, hd, tr), lambda i: (i // n_tab, 0, 0, i % n_tab)),
                      pl.BlockSpec((1, nh, 1, tr, hd), lambda i: (i // n_tab, 0, i % n_tab, 0, 0)),
                      pl.BlockSpec((1, nh, 1, hd, tr), lambda i: (i // n_tab, 0, i % n_tab, 0, 0))]
    else:
        in_specs = [col(0), col(1), col(2), col(3), tab, tab] + params
        args = (proj,) * 4
        out_shape.append(jax.ShapeDtypeStruct((t, w), BF16))
        out_specs.append(row)
    return pl.pallas_call(
        functools.partial(_ab_post_kernel, lc=lc, n_chunks=tr // lc, head_major=head_major),
        name="gmlp_rope",
        out_shape=out_shape,
        grid=(n_tiles,),
        in_specs=in_specs,
        out_specs=out_specs,
        compiler_params=_cparams("arbitrary"),
    )(*args, cos, sin, ln_g.reshape(1, w), ln_b.reshape(1, w), ws, bias)


def _rope_tables(pos):
    half = MOBA_HEAD_DIM // 2
    inv = jnp.power(ROPE_THETA, -jnp.arange(half, dtype=F32) / half)
    ang = pos.astype(F32)[:, None] * inv[None, :]
    cos = jnp.cos(ang)
    sin = jnp.sin(ang)
    cos_full = jnp.tile(jnp.concatenate([cos, cos], axis=-1), (1, MOBA_HEADS))
    sin_signed = jnp.tile(jnp.concatenate([-sin, sin], axis=-1), (1, MOBA_HEADS))
    return cos_full, sin_signed


def _top_rows(gate, n_sel):
    idx = _iota(gate.shape, 0)
    sel = jnp.zeros(gate.shape, F32)
    cur = gate
    for _ in range(n_sel):
        m = jnp.max(cur, axis=0, keepdims=True)
        first = jnp.min(jnp.where(cur == m, idx, gate.shape[0]), axis=0, keepdims=True)
        one = jnp.logical_and(idx == first, m > NEG_INF)
        sel = jnp.where(one, 1.0, sel)
        cur = jnp.where(one, NEG_INF, cur)
    return sel


def _moba_prompt_kernel(qt_ref, k_ref, vt_ref, km_ref, o_ref, sel_scr, *, group):
    qi = pl.program_id(2)
    n_heads = qt_ref.shape[1]
    d, tq = qt_ref.shape[2], qt_ref.shape[3]
    nb = km_ref.shape[2]
    blk = k_ref.shape[3]

    qs, init = [], []
    for hh in range(n_heads):
        q = qt_ref[0, hh]
        gate = _dot(km_ref[0, hh], q)
        gate = jnp.where(_iota(gate.shape, 0) < qi, gate, NEG_INF)
        sel = _top_rows(gate, min(MOBA_TOPK, nb))
        for r in range(nb):
            sel_scr[hh, r] = sel[r:r + 1, :]
        q = (q.astype(F32) * (MOBA_HEAD_DIM ** -0.5)).astype(BF16)
        qs.append(q)
        s = _dot(k_ref[0, hh, qi], q)
        s = jnp.where(_iota(s.shape, 0) <= _iota(s.shape, 1), s, NEG_INF)
        m0 = jnp.max(s, axis=0, keepdims=True)
        p = jnp.exp(s - m0)
        l0 = jnp.sum(p, axis=0, keepdims=True)
        init.append((m0, l0, _dot(vt_ref[0, hh, qi], p.astype(BF16))))

    def body(jg, carry):
        j0 = pl.multiple_of(jg * group, group)
        units = [(hh, g) for g in range(group) for hh in range(n_heads)]
        state = list(carry)
        score = lambda hh, g: _dot(k_ref[0, hh, j0 + g], qs[hh])
        s_next = score(*units[0])
        for u, (hh, g) in enumerate(units):
            s = s_next
            if u + 1 < len(units):
                s_next = score(*units[u + 1])
            s = jnp.where(sel_scr[hh, j0 + g] > 0.5, s, NEG_INF)
            m_u = jnp.max(s, axis=0, keepdims=True)
            m_ref = jnp.where(m_u > NEG_INF, m_u, 0.0)
            p = jnp.exp(s - m_ref)
            l_u = jnp.sum(p, axis=0, keepdims=True)
            pv = _dot(vt_ref[0, hh, j0 + g], p.astype(BF16))
            m, l, acc = state[hh]
            m_new = jnp.maximum(m, m_u)
            w_old = jnp.exp(m - m_new)
            w_u = jnp.exp(m_u - m_new)
            state[hh] = (m_new, l * w_old + l_u * w_u, acc * w_old + pv * w_u)
        return tuple(state)

    final = lax.fori_loop(0, (qi + group - 1) // group, body, tuple(init))
    for hh in range(n_heads):
        _, l, acc = final[hh]
        o_ref[0, hh] = (acc / l).astype(BF16)


def _moba_prompt(qt_hm, k_blk, vt_blk, km_hm):
    bsz, nh, d, seq = qt_hm.shape
    nb = k_blk.shape[2]
    tq = MOBA_BLOCK
    group = math.gcd(nb, MOBA_BLOCKS_PER_TRIP)
    hps = math.gcd(nh, MOBA_HEADS_PER_STEP)
    return pl.pallas_call(
        functools.partial(_moba_prompt_kernel, group=group),
        name="moba_prompt",
        out_shape=jax.ShapeDtypeStruct((bsz, nh, d, seq), BF16),
        grid=(bsz, nh // hps, seq // tq),
        in_specs=[pl.BlockSpec((1, hps, d, tq), lambda b, h, i: (b, h, 0, i)),
                  pl.BlockSpec((1, hps, nb, MOBA_BLOCK, d), lambda b, h, i: (b, h, 0, 0, 0)),
                  pl.BlockSpec((1, hps, nb, d, MOBA_BLOCK), lambda b, h, i: (b, h, 0, 0, 0)),
                  pl.BlockSpec((1, hps, nb, d), lambda b, h, i: (b, h, 0, 0))],
        out_specs=pl.BlockSpec((1, hps, d, tq), lambda b, h, i: (b, h, 0, i)),
        scratch_shapes=[pltpu.VMEM((hps, nb, 1, tq), F32)],
        compiler_params=_cparams("arbitrary", "arbitrary", "arbitrary"),
    )(qt_hm, k_blk, vt_blk, km_hm)


def _bdot(a, b):
    return lax.dot_general(a, b, (((2,), (1,)), ((0,), (0,))), preferred_element_type=F32)


def _bdot_nt(a, b):
    return lax.dot_general(a, b, (((2,), (2,)), ((0,), (0,))), preferred_element_type=F32)


def _moba_sample_kernel(pt_ref, q_ref, kn_ref, vn_ref, *rest, n_new, bps):
    page_refs = rest[:4 * bps]
    o_ref, o_scr, m_scr, l_scr, ks_scr = rest[4 * bps:]
    j = pl.program_id(1)
    nblk = pl.num_programs(1)
    q = q_ref[0]
    nh, rows, d = q.shape
    qs = (q.astype(F32) * (MOBA_HEAD_DIM ** -0.5)).astype(BF16)

    ones = jnp.ones((nh, rows, PAGE_SIZE), BF16)
    for blk in range(bps):
        kt = [page_refs[2 * blk + a][0] for a in range(2)]
        vt = [page_refs[2 * bps + 2 * blk + a][0] for a in range(2)]
        s = jnp.concatenate([_bdot(qs, k.astype(BF16)) for k in kt], axis=-1)
        m = jnp.max(s, axis=-1, keepdims=True)
        p = jnp.exp(s - m)
        l = jnp.sum(p, axis=-1, keepdims=True)
        pb = p.astype(BF16)
        o = (_bdot_nt(pb[:, :, :PAGE_SIZE], vt[0].astype(BF16))
             + _bdot_nt(pb[:, :, PAGE_SIZE:], vt[1].astype(BF16)))
        slot = j * bps + blk
        o_scr[slot] = o
        m_scr[slot] = jnp.broadcast_to(m, o.shape)
        l_scr[slot] = jnp.broadcast_to(l, o.shape)
        ksum = None
        for k in kt:
            hi = k.astype(BF16)
            mid = (k - hi.astype(F32)).astype(BF16)
            part = _bdot_nt(ones, hi) + _bdot_nt(ones, mid)
            ksum = part if ksum is None else ksum + part
        ks_scr[slot] = ksum

    @pl.when(j == nblk - 1)
    def _():
        nb = ks_scr.shape[0]
        kmean = (ks_scr[...] * (1.0 / MOBA_BLOCK)).astype(BF16).astype(F32)
        gate = jnp.sum(q.astype(F32)[None] * kmean, axis=-1, keepdims=True)
        sel = _top_rows(jnp.broadcast_to(gate, kmean.shape), min(MOBA_TOPK, nb)) > 0.5

        s_own = _bdot_nt(qs, kn_ref[0])
        causal = _iota(s_own.shape, 2) <= _iota(s_own.shape, 1)
        s_own = jnp.where(jnp.logical_and(causal, _iota(s_own.shape, 2) < n_new), s_own, NEG_INF)
        m_o = jnp.max(s_own, axis=-1, keepdims=True)
        p_o = jnp.exp(s_own - m_o)
        l_o = jnp.sum(p_o, axis=-1, keepdims=True)
        o_o = _bdot(p_o.astype(BF16), vn_ref[0])

        m_all = m_scr[...]
        m_top = jnp.maximum(m_o, jnp.max(jnp.where(sel, m_all, NEG_INF), axis=0))
        w = jnp.where(sel, jnp.exp(m_all - m_top[None]), 0.0)
        w_o = jnp.exp(m_o - m_top)
        l_tot = l_o * w_o + jnp.sum(w * l_scr[...], axis=0)
        o_tot = o_o * w_o + jnp.sum(w * o_scr[...], axis=0)
        o_ref[0] = o_tot / l_tot


def _moba_sample(page_table, q_hm, kn_hm, vn_hm, cache_kt, cache_vt, *, n_new):
    bsz, nh, rows, d = q_hm.shape
    n_pages = page_table.shape[1]
    pages_per_blk = MOBA_BLOCK // PAGE_SIZE
    assert pages_per_blk == 2 and n_pages % pages_per_blk == 0
    nblk = n_pages // pages_per_blk
    bps = math.gcd(nblk, MOBA_SAMPLE_BLOCKS_PER_STEP)
    pps = bps * pages_per_blk

    def page_spec(a):
        return pl.BlockSpec((1, nh, d, PAGE_SIZE), lambda b, j, pt: (pt[b * n_pages + pps * j + a], 0, 0, 0))

    seq_spec = pl.BlockSpec((1, nh, rows, d), lambda b, j, pt: (b, 0, 0, 0))
    part = pltpu.VMEM((nblk, nh, rows, d), F32)
    page_specs = [page_spec(a) for a in range(pps)]
    grid_spec = pltpu.PrefetchScalarGridSpec(
        num_scalar_prefetch=1,
        grid=(bsz, nblk // bps),
        in_specs=[seq_spec, seq_spec, seq_spec] + page_specs + page_specs,
        out_specs=seq_spec,
        scratch_shapes=[part, part, part, part])
    return pl.pallas_call(
        functools.partial(_moba_sample_kernel, n_new=n_new, bps=bps),
        name="moba_sample",
        out_shape=jax.ShapeDtypeStruct((bsz, nh, rows, d), F32),
        grid_spec=grid_spec,
        compiler_params=_cparams("arbitrary", "arbitrary"),
    )(page_table.reshape(-1), q_hm, kn_hm, vn_hm, *([cache_kt] * pps), *([cache_vt] * pps))


def _ssd_kernel(z_ref, xs_ref, bm_ref, cm_ref, dt_ref, conv0_ref, st0_ref, cw_ref, cb_ref, dtb_ref,
                alog_ref, dskip_ref, ng_ref, y_ref, stout_ref, convout_ref, st_scr, xbuf, yd_scr, *, lv):
    c = pl.program_id(1)
    last = pl.num_programs(1) - 1
    q = SSD_CHUNK
    hp = SSD_INNER
    gw = hp // SSD_GROUPS

    @pl.when(c == 0)
    def _():
        st_scr[...] = st0_ref[0]
        xbuf[0:8, :] = conv0_ref[0]

    def pad(x):
        if lv == q:
            return x
        return jnp.concatenate([x, jnp.zeros((q - lv, x.shape[1]), x.dtype)], axis=0)

    xbuf[8:8 + q, 0:hp] = pad(xs_ref[...])
    xbuf[8:8 + q, hp:hp + SSD_BC] = pad(bm_ref[...])
    xbuf[8:8 + q, hp + SSD_BC:] = pad(cm_ref[...])
    first = 8 - (SSD_CONV - 1)
    conv = cb_ref[...] + xbuf[first:first + q, :] * cw_ref[0:1, :]
    for tap in range(1, SSD_CONV):
        conv = conv + xbuf[first + tap:first + tap + q, :] * cw_ref[tap:tap + 1, :]

    @pl.when(c == last)
    def _():
        convout_ref[0] = xbuf[8 + lv - (SSD_CONV - 1):8 + lv, :]

    xbuf[0:8, :] = xbuf[q:q + 8, :]

    xbc = jax.nn.silu(conv)
    xs = xbc[:, :hp]
    bmat = xbc[:, hp:hp + SSD_BC]
    cmat = xbc[:, hp + SSD_BC:]

    dt = jax.nn.softplus(pad(dt_ref[...]) + dtb_ref[...])
    if lv != q:
        dt = jnp.where(_iota(dt.shape, 0) < lv, dt, 0.0)
    adt = dt * (-jnp.exp(alog_ref[...]))
    tri = (_iota((q, q), 1) <= _iota((q, q), 0)).astype(BF16)
    hi, mid, lo = _split3(adt)
    a_cum = _dot(tri, hi) + _dot(tri, mid) + _dot(tri, lo)
    a_cum_t = a_cum.T
    a_parts = _split3(a_cum)

    head_of_col = _iota((LANES, hp), 1) // SSD_HEAD_DIM
    e_cols = (head_of_col == _iota((LANES, hp), 0)).astype(BF16)
    e_time = (_iota((LANES, SSD_HEADS * q), 1) // q == _iota((LANES, SSD_HEADS * q), 0)).astype(BF16)
    dt_full = _dot_split(_split3(dt), e_cols)
    a_full = _dot_split(a_parts, e_cols)
    a_col = _dot_split(a_parts, e_time)
    a_last = a_full[q - 1:q, :]

    xdt = xs * dt_full
    xdt_b = xdt.astype(BF16)
    xdt_end = (xdt * jnp.exp(a_last - a_full)).astype(BF16)
    grow = jnp.exp(a_full)
    carry = jnp.exp(a_last)
    causal = _iota((q, q), 1) <= _iota((q, q), 0)
    low_half = _iota((q, LANES), 1) < SSD_HEAD_DIM
    heads_per_group = SSD_HEADS // SSD_GROUPS

    for g in range(SSD_GROUPS):
        cols = slice(g * gw, (g + 1) * gw)
        cc = cmat[:, g * SSD_STATE:(g + 1) * SSD_STATE].astype(BF16)
        bc = bmat[:, g * SSD_STATE:(g + 1) * SSD_STATE]
        cb = _dot_nt(cc, bc.astype(BF16))
        st_g = st_scr[:, cols]
        y_off = _dot(cc, st_g.astype(BF16)) * grow[:, cols]
        st_scr[:, cols] = st_g * carry[:, cols] + _dot(bc.T.astype(BF16), xdt_end[:, cols])
        for pair in range(heads_per_group // 2):
            h0 = g * heads_per_group + 2 * pair
            pcols = slice(h0 * SSD_HEAD_DIM, (h0 + 2) * SSD_HEAD_DIM)
            outs = []
            for h in (h0, h0 + 1):
                seg = a_col[:, h * q:(h + 1) * q] - a_cum_t[h:h + 1, :]
                decay = jnp.exp(jnp.where(causal, seg, NEG_INF))
                outs.append(_dot((cb * decay).astype(BF16), xdt_b[:, pcols]))
            yd_scr[:, pcols] = jnp.where(low_half, outs[0], outs[1]) + y_off[:, pcols.start - g * gw:pcols.stop - g * gw]

    y = yd_scr[...] + dskip_ref[...] * xs
    y = y * jax.nn.silu(pad(z_ref[...]))
    y = y * lax.rsqrt(jnp.mean(y * y, axis=-1, keepdims=True) + EPS) * ng_ref[...]
    y_ref[...] = y[:lv].astype(y_ref.dtype)

    @pl.when(c == last)
    def _():
        stout_ref[0] = st_scr[...]


def _ssd_scan(proj, conv0p, st0_t, conv_w, conv_b, dt_bias, a_log, dskip_full, norm_g, *, bsz, seq_len):
    lv = min(seq_len, SSD_CHUNK)
    nc = seq_len // lv
    hp = SSD_INNER
    t = proj.shape[0]
    blk = lambda w, cidx: pl.BlockSpec((lv, w), lambda b, c: (b * nc + c, cidx))
    full = lambda shape: pl.BlockSpec(shape, lambda b, c: (0,) * len(shape))
    per_seq = lambda shape: pl.BlockSpec((1,) + shape, lambda b, c: (b,) + (0,) * len(shape))
    return pl.pallas_call(
        functools.partial(_ssd_kernel, lv=lv),
        name="ssd_scan",
        out_shape=[jax.ShapeDtypeStruct((t, hp), BF16 if lv % 16 == 0 else F32),
                   jax.ShapeDtypeStruct((bsz, SSD_STATE, hp), F32),
                   jax.ShapeDtypeStruct((bsz, SSD_CONV - 1, SSD_CONV_DIM), F32)],
        grid=(bsz, nc),
        in_specs=[blk(hp, 0), blk(hp, 1), blk(SSD_BC, 2 * hp // SSD_BC), blk(SSD_BC, 2 * hp // SSD_BC + 1),
                  blk(LANES, (2 * hp + 2 * SSD_BC) // LANES),
                  per_seq((8, SSD_CONV_DIM)), per_seq((SSD_STATE, hp)),
                  full((SSD_CONV, SSD_CONV_DIM)), full((1, SSD_CONV_DIM)), full((1, LANES)), full((1, LANES)),
                  full((1, hp)), full((1, hp))],
        out_specs=[pl.BlockSpec((lv, hp), lambda b, c: (b * nc + c, 0)),
                   per_seq((SSD_STATE, hp)), per_seq((SSD_CONV - 1, SSD_CONV_DIM))],
        scratch_shapes=[pltpu.VMEM((SSD_STATE, hp), F32), pltpu.VMEM((SSD_CHUNK + 8, SSD_CONV_DIM), F32),
                        pltpu.VMEM((SSD_CHUNK, hp), F32)],
        compiler_params=_cparams("arbitrary", "arbitrary"),
    )(proj, proj, proj, proj, proj, conv0p, st0_t, conv_w, conv_b.reshape(1, -1), dt_bias, a_log,
      dskip_full, norm_g.reshape(1, -1))


def _argmax_rows(x):
    rows, n = x.shape
    idx = _iota((rows, n), 0)
    vs = [x[r:r + 8] for r in range(0, rows, 8)]
    ids = [idx[r:r + 8] for r in range(0, rows, 8)]
    while len(vs) > 1:
        nv, ni = [], []
        for p in range(0, len(vs) - 1, 2):
            nv.append(jnp.maximum(vs[p], vs[p + 1]))
            ni.append(jnp.where(vs[p] >= vs[p + 1], ids[p], ids[p + 1]))
        if len(vs) % 2:
            nv.append(vs[-1])
            ni.append(ids[-1])
        vs, ids = nv, ni
    m = jnp.max(vs[0], axis=0, keepdims=True)
    first = jnp.min(jnp.where(vs[0] == m, ids[0], rows), axis=0, keepdims=True)
    return m, first


def _extract_top(cur_scr, n_chain, k):
    rows, n = cur_scr.shape[1], cur_scr.shape[2]
    idx = _iota((rows, n), 0)
    slot = _iota((k, n), 0)

    def body(i, carry):
        out = []
        for c in range(n_chain):
            vals, picks = carry[c]
            cur = cur_scr[c]
            m, first = _argmax_rows(cur)
            cur_scr[c] = jnp.where(idx == first, NEG_INF, cur)
            out.append((jnp.where(slot == i, m, vals), jnp.where(slot == i, first, picks)))
        return tuple(out)

    init = tuple((jnp.zeros((k, n), F32), jnp.zeros((k, n), jnp.int32)) for _ in range(n_chain))
    return lax.fori_loop(0, k, body, init)


def _candidate_pieces(k):
    assert k == 16
    pieces = [(0, 1, 16, 16), (1, 2, 8, 8)]
    pieces += [(a, a + 1, 8, k // (a + 1)) for a in range(2, 8)]
    pieces.append((8, 16, 1, 1))
    return pieces


def _peer_route_kernel(qt_ref, sk_ref, alpha_ref, count_ref, beta_ref, rank_ref, cur_scr, cand_scr):
    k = PEER_TOPK
    hd = sk_ref.shape[3]
    nk = sk_ref.shape[2]
    pieces = _candidate_pieces(k)
    n_lt = qt_ref.shape[1] // LANES
    idx = _iota((nk, LANES), 0)

    scores = []
    for lt in range(n_lt):
        lanes = slice(lt * LANES, (lt + 1) * LANES)
        for c in range(2):
            s = _dot(sk_ref[0, c], qt_ref[c * hd:(c + 1) * hd, lanes])
            cur_scr[2 * lt + c] = s
            scores.append(s)
    tops = _extract_top(cur_scr, 2 * n_lt, k)

    for lt in range(n_lt):
        v1, v2 = tops[2 * lt][0], tops[2 * lt + 1][0]
        cands = []
        for a_lo, a_hi, n_b, n_valid in pieces:
            if a_hi - a_lo == 1:
                c = v1[a_lo:a_hi, :] + v2[0:n_b, :]
                cands.append(jnp.where(_iota(c.shape, 0) < n_valid, c, NEG_INF))
            else:
                cands.append(v1[a_lo:a_hi, :] + v2[0:1, :])
        cand_scr[lt] = jnp.concatenate(cands, axis=0)
    picked = _extract_top(cand_scr, n_lt, k)

    for lt in range(n_lt):
        lanes = slice(lt * LANES, (lt + 1) * LANES)
        (v1, rows1), (v2, rows2) = tops[2 * lt], tops[2 * lt + 1]
        best, best_rows = picked[lt]
        z = jnp.sum(jnp.exp(best - best[0:1, :]), axis=0, keepdims=True)
        count = jnp.zeros((nk, LANES), F32)
        rank2 = jnp.full((nk, LANES), float(k), F32)
        row = 0
        for a_lo, a_hi, n_b, _ in pieces:
            for a in range(a_lo, a_hi):
                in_piece = jnp.logical_and(best_rows >= row, best_rows < row + n_b)
                n_a = jnp.sum(jnp.where(in_piece, 1.0, 0.0), axis=0, keepdims=True)
                count = jnp.where(idx == rows1[a:a + 1, :], n_a, count)
                row += n_b
        for b in range(k):
            rank2 = jnp.where(idx == rows2[b:b + 1, :], float(b), rank2)
        alpha_ref[0, :, lanes] = jnp.exp(scores[2 * lt] - v1[0:1, :]) / z
        count_ref[0, :, lanes] = count
        beta_ref[0, :, lanes] = jnp.exp(scores[2 * lt + 1] - v2[0:1, :])
        rank_ref[0, :, lanes] = rank2


def _peer_route(q_t, subkeys, *, tt):
    t = q_t.shape[1]
    nh, _, nk, hd = subkeys.shape
    assert t % tt == 0 and tt % LANES == 0 and nk % 8 == 0
    out = jax.ShapeDtypeStruct((nh, nk, t), F32)
    ospec = pl.BlockSpec((1, nk, tt), lambda i, h: (h, 0, i))
    n_cand = sum((a_hi - a_lo) * n_b for a_lo, a_hi, n_b, _ in _candidate_pieces(PEER_TOPK))
    return pl.pallas_call(
        _peer_route_kernel,
        name="peer_route",
        out_shape=[out, out, out, out],
        grid=(t // tt, nh),
        in_specs=[pl.BlockSpec((2 * hd, tt), lambda i, h: (h, i)),
                  pl.BlockSpec((1, 2, nk, hd), lambda i, h: (h, 0, 0, 0))],
        out_specs=[ospec, ospec, ospec, ospec],
        scratch_shapes=[pltpu.VMEM((2 * (tt // LANES), nk, LANES), F32),
                        pltpu.VMEM((tt // LANES, n_cand, LANES), F32)],
        compiler_params=_cparams("arbitrary", "arbitrary"),
    )(q_t, subkeys)


def _peer_dense_kernel(ht_ref, u_ref, vt_ref, alpha_ref, count_ref, beta_ref, rank_ref, o_ref, aw_scr):
    e = pl.program_id(1)
    nh, nk, tt = beta_ref.shape
    n_sub, _, sub = vt_ref.shape
    per_sub = sub // nk
    ib = n_sub * per_sub

    @pl.when(e == 0)
    def _():
        o_ref[...] = jnp.zeros(o_ref.shape, F32)

    i0 = pl.multiple_of(e * ib, ib)
    s_next = _dot(u_ref[0:sub, :], ht_ref[...])
    acc = None
    for c in range(n_sub):
        s_cur = s_next
        if c + 1 < n_sub:
            s_next = _dot(u_ref[(c + 1) * sub:(c + 2) * sub, :], ht_ref[...])
        for lt in range(tt // LANES):
            lanes = slice(lt * LANES, (lt + 1) * LANES)
            counts = [count_ref[h, pl.ds(i0, ib), lanes] for h in range(nh)]
            alphas = [alpha_ref[h, pl.ds(i0, ib), lanes] for h in range(nh)]
            for kk in range(per_sub):
                ii = c * per_sub + kk
                rows = slice(kk * nk, (kk + 1) * nk)
                w = jnp.zeros((nk, LANES), F32)
                for h in range(nh):
                    hit = rank_ref[h, :, lanes] < counts[h][ii:ii + 1, :]
                    w = w + jnp.where(hit, beta_ref[h, :, lanes], 0.0) * alphas[h][ii:ii + 1, :]
                aw_scr[c, rows, lanes] = (w * _gelu(s_cur[rows, lanes])).astype(BF16)
        part = _dot(vt_ref[c], aw_scr[c])
        acc = part if acc is None else acc + part
    o_ref[...] += acc


def _peer_dense(h_t, u_tabs, vt_tiles, layer, route, *, tt, tile_subs):
    d, t = h_t.shape
    alpha, count, beta, rank = route
    nh, nk, _ = alpha.shape
    sub = vt_tiles.shape[3]
    assert t % tt == 0 and (nk * nk) % (tile_subs * sub) == 0 and (tile_subs * sub // nk) % 8 == 0
    te = tile_subs * sub
    rspec = pl.BlockSpec((nh, nk, tt), lambda i, e: (0, 0, i))
    return pl.pallas_call(
        _peer_dense_kernel,
        name="peer_dense",
        out_shape=jax.ShapeDtypeStruct((d, t), F32),
        grid=(t // tt, nk * nk // te),
        in_specs=[pl.BlockSpec((d, tt), lambda i, e: (0, i)),
                  pl.BlockSpec((None, te, d), lambda i, e: (layer, e, 0)),
                  pl.BlockSpec((None, tile_subs, d, sub), lambda i, e: (layer, e, 0, 0)),
                  rspec, rspec, rspec, rspec],
        out_specs=pl.BlockSpec((d, tt), lambda i, e: (0, i)),
        scratch_shapes=[pltpu.VMEM((tile_subs, sub, tt), BF16)],
        compiler_params=_cparams("arbitrary", "arbitrary"),
    )(h_t, u_tabs, vt_tiles, alpha, count, beta, rank)


def _prep_weights(norm_mix_g, mod_mix_w, mod_mix_b, norm_ffn_g, mod_ffn_w, mod_ffn_b,
                  w_in_ab, w_out_ab, gmlp_ln_g, gmlp_ln_b, gmlp_ws, gmlp_bs,
                  ssd_w_in, ssd_conv_w, ssd_conv_b, ssd_dt_bias, ssd_a_log, ssd_d, ssd_norm_g, ssd_w_out,
                  peer_wq, peer_subkeys, peer_u, peer_v, final_norm_g):
    depth = norm_mix_g.shape[0]
    d = norm_mix_g.shape[1]
    p = dict(depth=depth, norm_mix_g=norm_mix_g, norm_ffn_g=norm_ffn_g, final_norm_g=final_norm_g,
             gmlp_ln_g=gmlp_ln_g, gmlp_ln_b=gmlp_ln_b, gmlp_ws=gmlp_ws, gmlp_bs=gmlp_bs,
             ssd_conv_w=ssd_conv_w, ssd_conv_b=ssd_conv_b, ssd_norm_g=ssd_norm_g)
    mod_w, mod_b = [], []
    for i in range(depth):
        mod_w += [mod_mix_w[i], mod_ffn_w[i]]
        mod_b += [mod_mix_b[i], mod_ffn_b[i]]
    p["mod_w"] = jnp.stack(mod_w).astype(BF16)
    p["mod_b"] = jnp.stack(mod_b)[:, None, :]
    p["w_in_ab"] = w_in_ab.astype(BF16)
    p["w_out_ab"] = w_out_ab.astype(BF16)
    n_ssd = ssd_w_in.shape[0]
    z_w = ssd_w_in[:, :, :SSD_INNER]
    xbc_w = ssd_w_in[:, :, SSD_INNER:SSD_INNER + SSD_CONV_DIM]
    dt_w = ssd_w_in[:, :, SSD_INNER + SSD_CONV_DIM:]
    used = 2 * SSD_INNER + 2 * SSD_BC + LANES
    total = -(-used // SSD_IN_COL_TILE) * SSD_IN_COL_TILE
    p["ssd_w_in"] = jnp.concatenate(
        [z_w, xbc_w, dt_w, jnp.zeros((n_ssd, d, total - used + LANES - SSD_HEADS), F32)], axis=-1).astype(BF16)
    pad_heads = lambda a, v: jnp.concatenate([a, jnp.full((n_ssd, LANES - SSD_HEADS), v, F32)], axis=-1)[:, None, :]
    p["ssd_dt_bias"] = pad_heads(ssd_dt_bias, 0.0)
    p["ssd_a_log"] = pad_heads(ssd_a_log, 0.0)
    p["ssd_dskip"] = jnp.repeat(ssd_d, SSD_HEAD_DIM, axis=-1)[:, None, :]
    p["ssd_w_out"] = ssd_w_out.astype(BF16)
    p["peer_wq"] = peer_wq.astype(BF16)
    p["peer_subkeys"] = peer_subkeys.astype(BF16)
    p["peer_u"] = peer_u.astype(BF16)
    n_keys = peer_subkeys.shape[3]
    sub = PEER_FIRST_KEYS_PER_SUBTILE * n_keys
    p["peer_vt"] = peer_v.astype(BF16).reshape(depth, n_keys * n_keys // sub, sub, d).transpose(0, 1, 3, 2)
    return p


def _peer_layer(x, mod, seq_len, p, i, *, tm, final):
    q_t, h_t = _adaln_matmul(x, p["norm_ffn_g"][i], mod, seq_len, p["peer_wq"][i], tm=tm,
                             tn=p["peer_wq"].shape[2] // 2, out_dtype=BF16, feature_major=True)
    nk = p["peer_subkeys"].shape[3]
    route = _peer_route(q_t, p["peer_subkeys"][i], tt=PEER_TOKEN_TILE)
    y_t = _peer_dense(h_t, p["peer_u"], p["peer_vt"], i, route, tt=PEER_TOKEN_TILE,
                      tile_subs=PEER_SUBTILES_PER_STEP)
    return _residual_t(x, y_t, mod, seq_len, tm=tm, final_g=p["final_norm_g"] if final else None)


def _ab_layer(x, mod, bsz, seq_len, pos, past, p, i, j, *, tm):
    t, d = x.shape
    w = GMLP_GROUPS * (d // 16)
    proj = _adaln_matmul(x, p["norm_mix_g"][i], mod, seq_len, p["w_in_ab"][j], tm=tm,
                         tn=p["w_in_ab"].shape[2] // 2)
    cos, sin = _rope_tables(pos)
    lc = min(seq_len, GMLP_CHUNK)
    ws = p["gmlp_ws"][j][:, :lc, :lc]
    bias = jnp.repeat(p["gmlp_bs"][j][:, :lc].T, w // GMLP_GROUPS, axis=1)
    if seq_len < GMLP_CHUNK:
        reps = tm // seq_len
        ws = jnp.where(jnp.tril(jnp.ones((lc, lc), bool)), ws, 0.0)
        eye = jnp.eye(reps, dtype=F32)
        ws = jnp.einsum("ab,gts->gatbs", eye, ws).reshape(GMLP_GROUPS, tm, tm)
        bias = jnp.tile(bias, (reps, 1))
        cos = jnp.tile(cos, (reps, 1))
        sin = jnp.tile(sin, (reps, 1))
        tr, lc_eff = tm, tm
    else:
        tr, lc_eff = MOBA_BLOCK, lc
    post = _ab_post(proj, cos, sin, p["gmlp_ln_g"][j], p["gmlp_ln_b"][j], ws.astype(BF16), bias,
                    tr=tr, lc=lc_eff, head_major=past is None, bsz=bsz)
    a_out, vln, k_rot, kmean = post[:4]
    v_new = proj[:, 2 * w + 2 * w:2 * w + 3 * w]
    nh, hd = MOBA_HEADS, MOBA_HEAD_DIM
    if past is None:
        nb = seq_len // MOBA_BLOCK
        qt_hm, k_blk, vt_blk = post[4:]
        km_hm = kmean.reshape(bsz, nb, nh, hd).transpose(0, 2, 1, 3).astype(BF16)
        o_t = _moba_prompt(qt_hm, k_blk, vt_blk, km_hm)
        b_out = o_t.transpose(0, 3, 1, 2).reshape(t, w)
    else:
        q_rot = post[4]
        page_table, cache_k, cache_v = past
        rows = -(-seq_len // 16) * 16

        def head_major(a):
            a = a.astype(BF16).reshape(bsz, seq_len, nh, hd).transpose(0, 2, 1, 3)
            return jnp.pad(a, ((0, 0), (0, 0), (0, rows - seq_len), (0, 0)))

        b_out = _moba_sample(page_table, head_major(q_rot), head_major(k_rot), head_major(v_new),
                             cache_k.transpose(0, 2, 3, 1), cache_v.transpose(0, 2, 3, 1), n_new=seq_len)
        b_out = b_out[:, :, :seq_len].transpose(0, 2, 1, 3).reshape(t, w).astype(BF16)
    x = _proj_residual([a_out, b_out], p["w_out_ab"][j], x, mod, seq_len, tm=tm)
    lcv = min(seq_len, GMLP_CHUNK)
    gv = vln.reshape(bsz, seq_len, w)[:, seq_len - lcv:]
    return (x, k_rot.reshape(bsz, seq_len, nh, hd), v_new.reshape(bsz, seq_len, nh, hd), gv)


def _ssd_layer(x, mod, bsz, seq_len, ssm0, conv0, p, i, j, *, tm):
    t, d = x.shape
    proj = _adaln_matmul(x, p["norm_mix_g"][i], mod, seq_len, p["ssd_w_in"][j], tm=tm, tn=SSD_IN_COL_TILE)
    if conv0 is None:
        conv0 = jnp.zeros((bsz, SSD_CONV - 1, SSD_CONV_DIM), F32)
    if ssm0 is None:
        st0_t = jnp.zeros((bsz, SSD_STATE, SSD_INNER), F32)
    else:
        st0_t = ssm0.reshape(bsz, SSD_INNER, SSD_STATE).transpose(0, 2, 1)
    conv0p = jnp.pad(conv0, ((0, 0), (8 - (SSD_CONV - 1), 0), (0, 0)))
    y, st_t, conv_new = _ssd_scan(proj, conv0p, st0_t, p["ssd_conv_w"][j], p["ssd_conv_b"][j],
                                  p["ssd_dt_bias"][j], p["ssd_a_log"][j], p["ssd_dskip"][j],
                                  p["ssd_norm_g"][j], bsz=bsz, seq_len=seq_len)
    x = _proj_residual([y], p["ssd_w_out"][j], x, mod, seq_len, tm=tm)
    ssm_new = st_t.transpose(0, 2, 1).reshape(bsz, SSD_HEADS, SSD_HEAD_DIM, SSD_STATE)
    return x, ssm_new, conv_new


def _run_trunk(x, c, pos0, past_kv, ssm_past, conv_past, p):
    bsz, seq_len, d = x.shape
    t = bsz * seq_len
    tm = _token_tile(t, seq_len)
    assert t % tm == 0
    x = x.reshape(t, d)
    pos = pos0 + jnp.arange(seq_len, dtype=jnp.int32)
    mods = _mod_all(c, p["mod_w"], p["mod_b"])
    depth = p["depth"]
    k_rows, v_rows, gv_rows, ssm_new, conv_new = [], [], [], [], []
    for i in range(depth):
        j = i // 2
        if i % 2 == 0:
            past = None if past_kv is None else (past_kv[0], past_kv[1][j], past_kv[2][j])
            x, k_new, v_new, gv = _ab_layer(x, mods[2 * i], bsz, seq_len, pos, past, p, i, j, tm=tm)
            k_rows.append(k_new)
            v_rows.append(v_new)
            gv_rows.append(gv)
        else:
            x, s_new, cv_new = _ssd_layer(x, mods[2 * i], bsz, seq_len,
                                          None if ssm_past is None else ssm_past[j],
                                          None if conv_past is None else conv_past[j], p, i, j, tm=tm)
            ssm_new.append(s_new)
            conv_new.append(cv_new)
        x = _peer_layer(x, mods[2 * i + 1], seq_len, p, i, tm=tm, final=(i == depth - 1))
    y = x.reshape(bsz, seq_len, d)
    return y, jnp.stack(k_rows), jnp.stack(v_rows), jnp.stack(gv_rows), jnp.stack(ssm_new), jnp.stack(conv_new)


def kernel(x_prompt, x_sample, cache_k, cache_v, page_table, state_ssm, state_conv, c_prompt, c_sample,
           norm_mix_g, mod_mix_w, mod_mix_b, norm_ffn_g, mod_ffn_w, mod_ffn_b,
           w_in_ab, w_out_ab, gmlp_ln_g, gmlp_ln_b, gmlp_ws, gmlp_bs,
           ssd_w_in, ssd_conv_w, ssd_conv_b, ssd_dt_bias, ssd_a_log, ssd_d, ssd_norm_g, ssd_w_out,
           peer_wq, peer_subkeys, peer_u, peer_v, final_norm_g):
    p = _prep_weights(norm_mix_g, mod_mix_w, mod_mix_b, norm_ffn_g, mod_ffn_w, mod_ffn_b,
                      w_in_ab, w_out_ab, gmlp_ln_g, gmlp_ln_b, gmlp_ws, gmlp_bs,
                      ssd_w_in, ssd_conv_w, ssd_conv_b, ssd_dt_bias, ssd_a_log, ssd_d, ssd_norm_g, ssd_w_out,
                      peer_wq, peer_subkeys, peer_u, peer_v, final_norm_g)
    prompt = _run_trunk(x_prompt, c_prompt, 0, None, None, None, p)
    past_len = page_table.shape[1] * PAGE_SIZE
    sample = _run_trunk(x_sample, c_sample, past_len, (page_table, cache_k, cache_v),
                        state_ssm, state_conv, p)
    return (prompt[0], sample[0]) + prompt[1:] + sample[1:]
```

```python
import functools
import math

import jax
import jax.numpy as jnp
from jax import lax
from jax.experimental import pallas as pl
from jax.experimental.pallas import tpu as pltpu

F32 = jnp.float32
BF16 = jnp.bfloat16
NEG_INF = float("-inf")

EPS = 1e-6
LANES = 128
GMLP_GROUPS = 8
GMLP_CHUNK = 128
MOBA_HEADS = 8
MOBA_HEAD_DIM = 64
MOBA_BLOCK = 256
MOBA_TOPK = 3
PAGE_SIZE = 128
ROPE_THETA = 10000.0
SSD_HEADS = 32
SSD_HEAD_DIM = 64
SSD_GROUPS = 4
SSD_STATE = 128
SSD_CONV = 4
SSD_CHUNK = 128
SSD_INNER = SSD_HEADS * SSD_HEAD_DIM
SSD_BC = SSD_GROUPS * SSD_STATE
SSD_CONV_DIM = SSD_INNER + 2 * SSD_BC
PEER_HEADS = 8
PEER_TOPK = 16
VMEM_LIMIT_BYTES = 56 * 1024 * 1024

TOKEN_TILES = (1024, 512, 256)
SSD_IN_COL_TILE = 1792
PEER_TOKEN_TILE = 256
PEER_FIRST_KEYS_PER_SUBTILE = 2
PEER_SUBTILES_PER_STEP = 16
MOBA_HEADS_PER_STEP = 2
MOBA_BLOCKS_PER_TRIP = 4
MOBA_SAMPLE_BLOCKS_PER_STEP = 4


def _token_tile(t, seq_len):
    for c in TOKEN_TILES:
        if t % c == 0 and seq_len % c == 0:
            return c
    return TOKEN_TILES[-1]


def _cparams(*sem):
    return pltpu.CompilerParams(dimension_semantics=sem, vmem_limit_bytes=VMEM_LIMIT_BYTES)


def _dot(a, b):
    return jnp.dot(a, b, preferred_element_type=F32)


def _dot_nt(a, b):
    return lax.dot_general(a, b, (((1,), (1,)), ((), ())), preferred_element_type=F32)


def _iota(shape, dim):
    return lax.broadcasted_iota(jnp.int32, shape, dim)


def _gelu(x):
    return 0.5 * x * (1.0 + lax.erf(x * (2.0 ** -0.5)))


def _split3(x):
    hi = x.astype(BF16)
    r = x - hi.astype(F32)
    mid = r.astype(BF16)
    lo = (r - mid.astype(F32)).astype(BF16)
    return hi, mid, lo


def _dot_split(parts, e):
    hi, mid, lo = parts
    return _dot(hi, e) + _dot(mid, e) + _dot(lo, e)


def _mod_kernel(c_ref, w_ref, b_ref, o_ref):
    o_ref[0] = _dot(c_ref[...].astype(BF16), w_ref[0]) + b_ref[0]


def _mod_all(c, w4, b4):
    bsz, d = c.shape
    rows = -(-bsz // 16) * 16
    cp = jnp.pad(c, ((0, rows - bsz), (0, 0)))
    n = w4.shape[0]
    out = pl.pallas_call(
        _mod_kernel,
        name="adaln_mod",
        out_shape=jax.ShapeDtypeStruct((n, rows, 3 * d), F32),
        grid=(n, 3),
        in_specs=[pl.BlockSpec((rows, d), lambda i, j: (0, 0)),
                  pl.BlockSpec((1, d, d), lambda i, j: (i, 0, j)),
                  pl.BlockSpec((1, 1, d), lambda i, j: (i, 0, j))],
        out_specs=pl.BlockSpec((1, rows, d), lambda i, j: (i, 0, j)),
        compiler_params=_cparams("arbitrary", "arbitrary"),
    )(cp, w4, b4)
    return out[:, :bsz]


def _mod_rows(mod, seq_len, tm):
    bsz, w = mod.shape
    if seq_len % tm == 0:
        return mod[:, None, :], seq_len // tm
    assert tm % seq_len == 0 and (bsz * seq_len) % tm == 0
    return jnp.repeat(mod, seq_len, axis=0).reshape(bsz * seq_len // tm, tm, w), 1


def _adaln_mm_kernel(x_ref, g_ref, sh_ref, sc_ref, w_ref, o_ref, *rest, feature_major):
    if feature_major:
        ht_ref, h_scr = rest
    else:
        (h_scr,) = rest

    @pl.when(pl.program_id(1) == 0)
    def _():
        x = x_ref[...]
        y = x * lax.rsqrt(jnp.mean(x * x, axis=-1, keepdims=True) + EPS)
        y = y * g_ref[...]
        h = y * (1.0 + sc_ref[0]) + sh_ref[0]
        h_scr[...] = h.astype(BF16)
        if feature_major:
            ht_ref[...] = h.T.astype(BF16)

    out = _dot(h_scr[...], w_ref[...])
    o_ref[...] = (out.T if feature_major else out).astype(o_ref.dtype)


def _adaln_matmul(x, g, mod, seq_len, w, *, tm, tn, out_dtype=F32, feature_major=False):
    t, d = x.shape
    n = w.shape[1]
    assert t % tm == 0 and n % tn == 0
    mod3, tpg = _mod_rows(mod, seq_len, tm)
    r = mod3.shape[1]
    if feature_major:
        out_shape = [jax.ShapeDtypeStruct((n, t), out_dtype), jax.ShapeDtypeStruct((d, t), BF16)]
        out_specs = [pl.BlockSpec((tn, tm), lambda i, j: (j, i)), pl.BlockSpec((d, tm), lambda i, j: (0, i))]
    else:
        out_shape = [jax.ShapeDtypeStruct((t, n), out_dtype)]
        out_specs = [pl.BlockSpec((tm, tn), lambda i, j: (i, j))]
    res = pl.pallas_call(
        functools.partial(_adaln_mm_kernel, feature_major=feature_major),
        name="adaln_matmul",
        out_shape=out_shape,
        grid=(t // tm, n // tn),
        in_specs=[pl.BlockSpec((tm, d), lambda i, j: (i, 0)),
                  pl.BlockSpec((1, d), lambda i, j: (0, 0)),
                  pl.BlockSpec((1, r, d), lambda i, j: (i // tpg, 0, 0)),
                  pl.BlockSpec((1, r, d), lambda i, j: (i // tpg, 0, 1)),
                  pl.BlockSpec((d, tn), lambda i, j: (0, j))],
        out_specs=out_specs,
        scratch_shapes=[pltpu.VMEM((tm, d), BF16)],
        compiler_params=_cparams("arbitrary", "arbitrary"),
    )(x, g.reshape(1, d), mod3, mod3, w)
    return res if feature_major else res[0]


def _proj_res_kernel(*refs, ks):
    lhs = refs[:len(ks)]
    w_ref, x_ref, gate_ref, o_ref = refs[len(ks):]
    acc = None
    off = 0
    for l_ref, k in zip(lhs, ks):
        part = _dot(l_ref[...].astype(BF16), w_ref[off:off + k, :])
        acc = part if acc is None else acc + part
        off += k
    o_ref[...] = x_ref[...] + gate_ref[0] * acc


def _proj_residual(lhs, w, x, mod, seq_len, *, tm):
    t, d = x.shape
    ks = tuple(l.shape[1] for l in lhs)
    mod3, tpg = _mod_rows(mod, seq_len, tm)
    r = mod3.shape[1]
    in_specs = [pl.BlockSpec((tm, k), lambda i: (i, 0)) for k in ks]
    in_specs += [pl.BlockSpec((sum(ks), d), lambda i: (0, 0)),
                 pl.BlockSpec((tm, d), lambda i: (i, 0)),
                 pl.BlockSpec((1, r, d), lambda i: (i // tpg, 0, 2))]
    return pl.pallas_call(
        functools.partial(_proj_res_kernel, ks=ks),
        name="proj_residual",
        out_shape=jax.ShapeDtypeStruct((t, d), F32),
        grid=(t // tm,),
        in_specs=in_specs,
        out_specs=pl.BlockSpec((tm, d), lambda i: (i, 0)),
        compiler_params=_cparams("arbitrary"),
    )(*lhs, w, x, mod3)


def _res_kernel(x_ref, yt_ref, gate_ref, *rest, final_norm):
    x2 = x_ref[...] + gate_ref[0] * yt_ref[...].T
    if final_norm:
        g_ref, o_ref = rest
        o_ref[...] = x2 * lax.rsqrt(jnp.mean(x2 * x2, axis=-1, keepdims=True) + EPS) * g_ref[...]
    else:
        (o_ref,) = rest
        o_ref[...] = x2


def _residual_t(x, y_t, mod, seq_len, *, tm, final_g=None):
    t, d = x.shape
    mod3, tpg = _mod_rows(mod, seq_len, tm)
    r = mod3.shape[1]
    in_specs = [pl.BlockSpec((tm, d), lambda i: (i, 0)),
                pl.BlockSpec((d, tm), lambda i: (0, i)),
                pl.BlockSpec((1, r, d), lambda i: (i // tpg, 0, 2))]
    args = [x, y_t, mod3]
    if final_g is not None:
        in_specs.append(pl.BlockSpec((1, d), lambda i: (0, 0)))
        args.append(final_g.reshape(1, d))
    return pl.pallas_call(
        functools.partial(_res_kernel, final_norm=final_g is not None),
        name="peer_residual",
        out_shape=jax.ShapeDtypeStruct((t, d), F32),
        grid=(t // tm,),
        in_specs=in_specs,
        out_specs=pl.BlockSpec((tm, d), lambda i: (i, 0)),
        compiler_params=_cparams("arbitrary"),
    )(*args)


def _rope(x, cos, sin_signed):
    lane = _iota(x.shape, 1) % MOBA_HEAD_DIM
    half = MOBA_HEAD_DIM // 2
    width = x.shape[1]
    partner = jnp.where(lane < half, pltpu.roll(x, width - half, 1), pltpu.roll(x, half, 1))
    return x * cos + partner * sin_signed


def _ab_post_kernel(u_ref, v_ref, q_ref, k_ref, *rest, lc, n_chunks, head_major):
    if head_major:
        (vb_ref, cos_ref, sin_ref, lng_ref, lnb_ref, ws_ref, bias_ref,
         a_ref, vln_ref, kr_ref, km_ref, qt_ref, khm_ref, vt_ref) = rest
    else:
        (cos_ref, sin_ref, lng_ref, lnb_ref, ws_ref, bias_ref,
         a_ref, vln_ref, kr_ref, km_ref, qr_ref) = rest
    u = _gelu(u_ref[...])
    v = _gelu(v_ref[...])
    mu = jnp.mean(v, axis=-1, keepdims=True)
    var = jnp.mean(jnp.square(v - mu), axis=-1, keepdims=True)
    vln = (v - mu) * lax.rsqrt(var + EPS) * lng_ref[...] + lnb_ref[...]
    vln_ref[...] = vln

    width = vln.shape[1]
    lane_group = _iota((lc, width), 1) // (width // GMLP_GROUPS)
    causal = _iota((lc, lc), 1) <= _iota((lc, lc), 0)
    w_tril = [jnp.where(causal, ws_ref[g], jnp.zeros((), BF16)) for g in range(GMLP_GROUPS)]
    for c in range(n_chunks):
        rows = slice(c * lc, (c + 1) * lc)
        vc = vln[rows].astype(BF16)
        mix = jnp.zeros((lc, width), F32)
        for g in range(GMLP_GROUPS):
            mix = mix + jnp.where(lane_group == g, _dot(w_tril[g], vc), 0.0)
        a_ref[rows, :] = (u[rows] * (mix + bias_ref[...])).astype(BF16)

    cos = cos_ref[...]
    sin = sin_ref[...]
    qr = _rope(q_ref[...], cos, sin)
    kr = _rope(k_ref[...], cos, sin)
    kr_ref[...] = kr
    km_ref[0] = jnp.mean(kr, axis=0, keepdims=True)
    if head_major:
        tr = qr.shape[0]
        qt_ref[0] = qr.T.reshape(MOBA_HEADS, MOBA_HEAD_DIM, tr).astype(BF16)
        vt_ref[0, :, 0] = vb_ref[...].T.reshape(MOBA_HEADS, MOBA_HEAD_DIM, tr).astype(BF16)
        for h in range(MOBA_HEADS):
            khm_ref[0, h, 0] = kr[:, h * MOBA_HEAD_DIM:(h + 1) * MOBA_HEAD_DIM].astype(BF16)
    else:
        qr_ref[...] = qr.astype(BF16)


def _ab_post(proj, cos, sin, ln_g, ln_b, ws, bias, *, tr, lc, head_major, bsz):
    t = proj.shape[0]
    w = ln_g.shape[0]
    n_tab = cos.shape[0] // tr
    n_tiles = t // tr
    nh, hd = MOBA_HEADS, MOBA_HEAD_DIM
    col = lambda c: pl.BlockSpec((tr, w), lambda i: (i, c))
    tab = pl.BlockSpec((tr, w), lambda i: (i % n_tab, 0))
    row = pl.BlockSpec((tr, w), lambda i: (i, 0))
    params = [pl.BlockSpec((1, w), lambda i: (0, 0)),
              pl.BlockSpec((1, w), lambda i: (0, 0)),
              pl.BlockSpec((GMLP_GROUPS, lc, lc), lambda i: (0, 0, 0)),
              pl.BlockSpec((lc, w), lambda i: (0, 0))]
    out_shape = [jax.ShapeDtypeStruct((t, w), BF16),
                 jax.ShapeDtypeStruct((t, w), F32),
                 jax.ShapeDtypeStruct((t, w), F32),
                 jax.ShapeDtypeStruct((n_tiles, 1, w), F32)]
    out_specs = [row, row, row, pl.BlockSpec((1, 1, w), lambda i: (i, 0, 0))]
    if head_major:
        assert tr == MOBA_BLOCK
        in_specs = [col(0), col(1), col(2), col(3), col(4), tab, tab] + params
        args = (proj,) * 5
        out_shape += [jax.ShapeDtypeStruct((bsz, nh, hd, n_tab * tr), BF16),
                      jax.ShapeDtypeStruct((bsz, nh, n_tab, tr, hd), BF16),
                      jax.ShapeDtypeStruct((bsz, nh, n_tab, hd, tr), BF16)]
        out_specs += [pl.BlockSpec((1, nh, hd, tr), lambda i: (i // n_tab, 0, 0, i % n_tab)),
                      pl.BlockSpec((1, nh, 1, tr, hd), lambda i: (i // n_tab, 0, i % n_tab, 0, 0)),
                      pl.BlockSpec((1, nh, 1, hd, tr), lambda i: (i // n_tab, 0, i % n_tab, 0, 0))]
    else:
        in_specs = [col(0), col(1), col(2), col(3), tab, tab] + params
        args = (proj,) * 4
        out_shape.append(jax.ShapeDtypeStruct((t, w), BF16))
        out_specs.append(row)
    return pl.pallas_call(
        functools.partial(_ab_post_kernel, lc=lc, n_chunks=tr // lc, head_major=head_major),
        name="gmlp_rope",
        out_shape=out_shape,
        grid=(n_tiles,),
        in_specs=in_specs,
        out_specs=out_specs,
        compiler_params=_cparams("arbitrary"),
    )(*args, cos, sin, ln_g.reshape(1, w), ln_b.reshape(1, w), ws, bias)


def _rope_tables(pos):
    half = MOBA_HEAD_DIM // 2
    inv = jnp.power(ROPE_THETA, -jnp.arange(half, dtype=F32) / half)
    ang = pos.astype(F32)[:, None] * inv[None, :]
    cos = jnp.cos(ang)
    sin = jnp.sin(ang)
    cos_full = jnp.tile(jnp.concatenate([cos, cos], axis=-1), (1, MOBA_HEADS))
    sin_signed = jnp.tile(jnp.concatenate([-sin, sin], axis=-1), (1, MOBA_HEADS))
    return cos_full, sin_signed


def _top_rows(gate, n_sel):
    idx = _iota(gate.shape, 0)
    sel = jnp.zeros(gate.shape, F32)
    cur = gate
    for _ in range(n_sel):
        m = jnp.max(cur, axis=0, keepdims=True)
        first = jnp.min(jnp.where(cur == m, idx, gate.shape[0]), axis=0, keepdims=True)
        one = jnp.logical_and(idx == first, m > NEG_INF)
        sel = jnp.where(one, 1.0, sel)
        cur = jnp.where(one, NEG_INF, cur)
    return sel


def _moba_prompt_kernel(qt_ref, k_ref, vt_ref, km_ref, o_ref, sel_scr, *, group):
    qi = pl.program_id(2)
    n_heads = qt_ref.shape[1]
    d, tq = qt_ref.shape[2], qt_ref.shape[3]
    nb = km_ref.shape[2]
    blk = k_ref.shape[3]

    qs, init = [], []
    for hh in range(n_heads):
        q = qt_ref[0, hh]
        gate = _dot(km_ref[0, hh], q)
        gate = jnp.where(_iota(gate.shape, 0) < qi, gate, NEG_INF)
        sel = _top_rows(gate, min(MOBA_TOPK, nb))
        for r in range(nb):
            sel_scr[hh, r] = sel[r:r + 1, :]
        q = (q.astype(F32) * (MOBA_HEAD_DIM ** -0.5)).astype(BF16)
        qs.append(q)
        s = _dot(k_ref[0, hh, qi], q)
        s = jnp.where(_iota(s.shape, 0) <= _iota(s.shape, 1), s, NEG_INF)
        m0 = jnp.max(s, axis=0, keepdims=True)
        p = jnp.exp(s - m0)
        l0 = jnp.sum(p, axis=0, keepdims=True)
        init.append((m0, l0, _dot(vt_ref[0, hh, qi], p.astype(BF16))))

    def body(jg, carry):
        j0 = pl.multiple_of(jg * group, group)
        units = [(hh, g) for g in range(group) for hh in range(n_heads)]
        state = list(carry)
        score = lambda hh, g: _dot(k_ref[0, hh, j0 + g], qs[hh])
        s_next = score(*units[0])
        for u, (hh, g) in enumerate(units):
            s = s_next
            if u + 1 < len(units):
                s_next = score(*units[u + 1])
            s = jnp.where(sel_scr[hh, j0 + g] > 0.5, s, NEG_INF)
            m_u = jnp.max(s, axis=0, keepdims=True)
            m_ref = jnp.where(m_u > NEG_INF, m_u, 0.0)
            p = jnp.exp(s - m_ref)
            l_u = jnp.sum(p, axis=0, keepdims=True)
            pv = _dot(vt_ref[0, hh, j0 + g], p.astype(BF16))
            m, l, acc = state[hh]
            m_new = jnp.maximum(m, m_u)
            w_old = jnp.exp(m - m_new)
            w_u = jnp.exp(m_u - m_new)
            state[hh] = (m_new, l * w_old + l_u * w_u, acc * w_old + pv * w_u)
        return tuple(state)

    final = lax.fori_loop(0, (qi + group - 1) // group, body, tuple(init))
    for hh in range(n_heads):
        _, l, acc = final[hh]
        o_ref[0, hh] = (acc / l).astype(BF16)


def _moba_prompt(qt_hm, k_blk, vt_blk, km_hm):
    bsz, nh, d, seq = qt_hm.shape
    nb = k_blk.shape[2]
    tq = MOBA_BLOCK
    group = math.gcd(nb, MOBA_BLOCKS_PER_TRIP)
    hps = math.gcd(nh, MOBA_HEADS_PER_STEP)
    return pl.pallas_call(
        functools.partial(_moba_prompt_kernel, group=group),
        name="moba_prompt",
        out_shape=jax.ShapeDtypeStruct((bsz, nh, d, seq), BF16),
        grid=(bsz, nh // hps, seq // tq),
        in_specs=[pl.BlockSpec((1, hps, d, tq), lambda b, h, i: (b, h, 0, i)),
                  pl.BlockSpec((1, hps, nb, MOBA_BLOCK, d), lambda b, h, i: (b, h, 0, 0, 0)),
                  pl.BlockSpec((1, hps, nb, d, MOBA_BLOCK), lambda b, h, i: (b, h, 0, 0, 0)),
                  pl.BlockSpec((1, hps, nb, d), lambda b, h, i: (b, h, 0, 0))],
        out_specs=pl.BlockSpec((1, hps, d, tq), lambda b, h, i: (b, h, 0, i)),
        scratch_shapes=[pltpu.VMEM((hps, nb, 1, tq), F32)],
        compiler_params=_cparams("arbitrary", "arbitrary", "arbitrary"),
    )(qt_hm, k_blk, vt_blk, km_hm)


def _bdot(a, b):
    return lax.dot_general(a, b, (((2,), (1,)), ((0,), (0,))), preferred_element_type=F32)


def _bdot_nt(a, b):
    return lax.dot_general(a, b, (((2,), (2,)), ((0,), (0,))), preferred_element_type=F32)


def _moba_sample_kernel(pt_ref, q_ref, kn_ref, vn_ref, *rest, n_new, bps):
    page_refs = rest[:4 * bps]
    o_ref, o_scr, m_scr, l_scr, ks_scr = rest[4 * bps:]
    j = pl.program_id(1)
    nblk = pl.num_programs(1)
    q = q_ref[0]
    nh, rows, d = q.shape
    qs = (q.astype(F32) * (MOBA_HEAD_DIM ** -0.5)).astype(BF16)

    ones = jnp.ones((nh, rows, PAGE_SIZE), BF16)
    for blk in range(bps):
        kt = [page_refs[2 * blk + a][0] for a in range(2)]
        vt = [page_refs[2 * bps + 2 * blk + a][0] for a in range(2)]
        s = jnp.concatenate([_bdot(qs, k.astype(BF16)) for k in kt], axis=-1)
        m = jnp.max(s, axis=-1, keepdims=True)
        p = jnp.exp(s - m)
        l = jnp.sum(p, axis=-1, keepdims=True)
        pb = p.astype(BF16)
        o = (_bdot_nt(pb[:, :, :PAGE_SIZE], vt[0].astype(BF16))
             + _bdot_nt(pb[:, :, PAGE_SIZE:], vt[1].astype(BF16)))
        slot = j * bps + blk
        o_scr[slot] = o
        m_scr[slot] = jnp.broadcast_to(m, o.shape)
        l_scr[slot] = jnp.broadcast_to(l, o.shape)
        ksum = None
        for k in kt:
            hi = k.astype(BF16)
            mid = (k - hi.astype(F32)).astype(BF16)
            part = _bdot_nt(ones, hi) + _bdot_nt(ones, mid)
            ksum = part if ksum is None else ksum + part
        ks_scr[slot] = ksum

    @pl.when(j == nblk - 1)
    def _():
        nb = ks_scr.shape[0]
        kmean = (ks_scr[...] * (1.0 / MOBA_BLOCK)).astype(BF16).astype(F32)
        gate = jnp.sum(q.astype(F32)[None] * kmean, axis=-1, keepdims=True)
        sel = _top_rows(jnp.broadcast_to(gate, kmean.shape), min(MOBA_TOPK, nb)) > 0.5

        s_own = _bdot_nt(qs, kn_ref[0])
        causal = _iota(s_own.shape, 2) <= _iota(s_own.shape, 1)
        s_own = jnp.where(jnp.logical_and(causal, _iota(s_own.shape, 2) < n_new), s_own, NEG_INF)
        m_o = jnp.max(s_own, axis=-1, keepdims=True)
        p_o = jnp.exp(s_own - m_o)
        l_o = jnp.sum(p_o, axis=-1, keepdims=True)
        o_o = _bdot(p_o.astype(BF16), vn_ref[0])

        m_all = m_scr[...]
        m_top = jnp.maximum(m_o, jnp.max(jnp.where(sel, m_all, NEG_INF), axis=0))
        w = jnp.where(sel, jnp.exp(m_all - m_top[None]), 0.0)
        w_o = jnp.exp(m_o - m_top)
        l_tot = l_o * w_o + jnp.sum(w * l_scr[...], axis=0)
        o_tot = o_o * w_o + jnp.sum(w * o_scr[...], axis=0)
        o_ref[0] = o_tot / l_tot


def _moba_sample(page_table, q_hm, kn_hm, vn_hm, cache_kt, cache_vt, *, n_new):
    bsz, nh, rows, d = q_hm.shape
    n_pages = page_table.shape[1]
    pages_per_blk = MOBA_BLOCK // PAGE_SIZE
    assert pages_per_blk == 2 and n_pages % pages_per_blk == 0
    nblk = n_pages // pages_per_blk
    bps = math.gcd(nblk, MOBA_SAMPLE_BLOCKS_PER_STEP)
    pps = bps * pages_per_blk

    def page_spec(a):
        return pl.BlockSpec((1, nh, d, PAGE_SIZE), lambda b, j, pt: (pt[b * n_pages + pps * j + a], 0, 0, 0))

    seq_spec = pl.BlockSpec((1, nh, rows, d), lambda b, j, pt: (b, 0, 0, 0))
    part = pltpu.VMEM((nblk, nh, rows, d), F32)
    page_specs = [page_spec(a) for a in range(pps)]
    grid_spec = pltpu.PrefetchScalarGridSpec(
        num_scalar_prefetch=1,
        grid=(bsz, nblk // bps),
        in_specs=[seq_spec, seq_spec, seq_spec] + page_specs + page_specs,
        out_specs=seq_spec,
        scratch_shapes=[part, part, part, part])
    return pl.pallas_call(
        functools.partial(_moba_sample_kernel, n_new=n_new, bps=bps),
        name="moba_sample",
        out_shape=jax.ShapeDtypeStruct((bsz, nh, rows, d), F32),
        grid_spec=grid_spec,
        compiler_params=_cparams("arbitrary", "arbitrary"),
    )(page_table.reshape(-1), q_hm, kn_hm, vn_hm, *([cache_kt] * pps), *([cache_vt] * pps))


def _ssd_kernel(z_ref, xs_ref, bm_ref, cm_ref, dt_ref, conv0_ref, st0_ref, cw_ref, cb_ref, dtb_ref,
                alog_ref, dskip_ref, ng_ref, y_ref, stout_ref, convout_ref, st_scr, xbuf, yd_scr, *, lv):
    c = pl.program_id(1)
    last = pl.num_programs(1) - 1
    q = SSD_CHUNK
    hp = SSD_INNER
    gw = hp // SSD_GROUPS

    @pl.when(c == 0)
    def _():
        st_scr[...] = st0_ref[0]
        xbuf[0:8, :] = conv0_ref[0]

    def pad(x):
        if lv == q:
            return x
        return jnp.concatenate([x, jnp.zeros((q - lv, x.shape[1]), x.dtype)], axis=0)

    xbuf[8:8 + q, 0:hp] = pad(xs_ref[...])
    xbuf[8:8 + q, hp:hp + SSD_BC] = pad(bm_ref[...])
    xbuf[8:8 + q, hp + SSD_BC:] = pad(cm_ref[...])
    first = 8 - (SSD_CONV - 1)
    conv = cb_ref[...] + xbuf[first:first + q, :] * cw_ref[0:1, :]
    for tap in range(1, SSD_CONV):
        conv = conv + xbuf[first + tap:first + tap + q, :] * cw_ref[tap:tap + 1, :]

    @pl.when(c == last)
    def _():
        convout_ref[0] = xbuf[8 + lv - (SSD_CONV - 1):8 + lv, :]

    xbuf[0:8, :] = xbuf[q:q + 8, :]

    xbc = jax.nn.silu(conv)
    xs = xbc[:, :hp]
    bmat = xbc[:, hp:hp + SSD_BC]
    cmat = xbc[:, hp + SSD_BC:]

    dt = jax.nn.softplus(pad(dt_ref[...]) + dtb_ref[...])
    if lv != q:
        dt = jnp.where(_iota(dt.shape, 0) < lv, dt, 0.0)
    adt = dt * (-jnp.exp(alog_ref[...]))
    tri = (_iota((q, q), 1) <= _iota((q, q), 0)).astype(BF16)
    hi, mid, lo = _split3(adt)
    a_cum = _dot(tri, hi) + _dot(tri, mid) + _dot(tri, lo)
    a_cum_t = a_cum.T
    a_parts = _split3(a_cum)

    head_of_col = _iota((LANES, hp), 1) // SSD_HEAD_DIM
    e_cols = (head_of_col == _iota((LANES, hp), 0)).astype(BF16)
    e_time = (_iota((LANES, SSD_HEADS * q), 1) // q == _iota((LANES, SSD_HEADS * q), 0)).astype(BF16)
    dt_full = _dot_split(_split3(dt), e_cols)
    a_full = _dot_split(a_parts, e_cols)
    a_col = _dot_split(a_parts, e_time)
    a_last = a_full[q - 1:q, :]

    xdt = xs * dt_full
    xdt_b = xdt.astype(BF16)
    xdt_end = (xdt * jnp.exp(a_last - a_full)).astype(BF16)
    grow = jnp.exp(a_full)
    carry = jnp.exp(a_last)
    causal = _iota((q, q), 1) <= _iota((q, q), 0)
    low_half = _iota((q, LANES), 1) < SSD_HEAD_DIM
    heads_per_group = SSD_HEADS // SSD_GROUPS

    for g in range(SSD_GROUPS):
        cols = slice(g * gw, (g + 1) * gw)
        cc = cmat[:, g * SSD_STATE:(g + 1) * SSD_STATE].astype(BF16)
        bc = bmat[:, g * SSD_STATE:(g + 1) * SSD_STATE]
        cb = _dot_nt(cc, bc.astype(BF16))
        st_g = st_scr[:, cols]
        y_off = _dot(cc, st_g.astype(BF16)) * grow[:, cols]
        st_scr[:, cols] = st_g * carry[:, cols] + _dot(bc.T.astype(BF16), xdt_end[:, cols])
        for pair in range(heads_per_group // 2):
            h0 = g * heads_per_group + 2 * pair
            pcols = slice(h0 * SSD_HEAD_DIM, (h0 + 2) * SSD_HEAD_DIM)
            outs = []
            for h in (h0, h0 + 1):
                seg = a_col[:, h * q:(h + 1) * q] - a_cum_t[h:h + 1, :]
                decay = jnp.exp(jnp.where(causal, seg, NEG_INF))
                outs.append(_dot((cb * decay).astype(BF16), xdt_b[:, pcols]))
            yd_scr[:, pcols] = jnp.where(low_half, outs[0], outs[1]) + y_off[:, pcols.start - g * gw:pcols.stop - g * gw]

    y = yd_scr[...] + dskip_ref[...] * xs
    y = y * jax.nn.silu(pad(z_ref[...]))
    y = y * lax.rsqrt(jnp.mean(y * y, axis=-1, keepdims=True) + EPS) * ng_ref[...]
    y_ref[...] = y[:lv].astype(y_ref.dtype)

    @pl.when(c == last)
    def _():
        stout_ref[0] = st_scr[...]


def _ssd_scan(proj, conv0p, st0_t, conv_w, conv_b, dt_bias, a_log, dskip_full, norm_g, *, bsz, seq_len):
    lv = min(seq_len, SSD_CHUNK)
    nc = seq_len // lv
    hp = SSD_INNER
    t = proj.shape[0]
    blk = lambda w, cidx: pl.BlockSpec((lv, w), lambda b, c: (b * nc + c, cidx))
    full = lambda shape: pl.BlockSpec(shape, lambda b, c: (0,) * len(shape))
    per_seq = lambda shape: pl.BlockSpec((1,) + shape, lambda b, c: (b,) + (0,) * len(shape))
    return pl.pallas_call(
        functools.partial(_ssd_kernel, lv=lv),
        name="ssd_scan",
        out_shape=[jax.ShapeDtypeStruct((t, hp), BF16 if lv % 16 == 0 else F32),
                   jax.ShapeDtypeStruct((bsz, SSD_STATE, hp), F32),
                   jax.ShapeDtypeStruct((bsz, SSD_CONV - 1, SSD_CONV_DIM), F32)],
        grid=(bsz, nc),
        in_specs=[blk(hp, 0), blk(hp, 1), blk(SSD_BC, 2 * hp // SSD_BC), blk(SSD_BC, 2 * hp // SSD_BC + 1),
                  blk(LANES, (2 * hp + 2 * SSD_BC) // LANES),
                  per_seq((8, SSD_CONV_DIM)), per_seq((SSD_STATE, hp)),
                  full((SSD_CONV, SSD_CONV_DIM)), full((1, SSD_CONV_DIM)), full((1, LANES)), full((1, LANES)),
                  full((1, hp)), full((1, hp))],
        out_specs=[pl.BlockSpec((lv, hp), lambda b, c: (b * nc + c, 0)),
                   per_seq((SSD_STATE, hp)), per_seq((SSD_CONV - 1, SSD_CONV_DIM))],
        scratch_shapes=[pltpu.VMEM((SSD_STATE, hp), F32), pltpu.VMEM((SSD_CHUNK + 8, SSD_CONV_DIM), F32),
                        pltpu.VMEM((SSD_CHUNK, hp), F32)],
        compiler_params=_cparams("arbitrary", "arbitrary"),
    )(proj, proj, proj, proj, proj, conv0p, st0_t, conv_w, conv_b.reshape(1, -1), dt_bias, a_log,
      dskip_full, norm_g.reshape(1, -1))


def _argmax_rows(x):
    rows, n = x.shape
    idx = _iota((rows, n), 0)
    vs = [x[r:r + 8] for r in range(0, rows, 8)]
    ids = [idx[r:r + 8] for r in range(0, rows, 8)]
    while len(vs) > 1:
        nv, ni = [], []
        for p in range(0, len(vs) - 1, 2):
            nv.append(jnp.maximum(vs[p], vs[p + 1]))
            ni.append(jnp.where(vs[p] >= vs[p + 1], ids[p], ids[p + 1]))
        if len(vs) % 2:
            nv.append(vs[-1])
            ni.append(ids[-1])
        vs, ids = nv, ni
    m = jnp.max(vs[0], axis=0, keepdims=True)
    first = jnp.min(jnp.where(vs[0] == m, ids[0], rows), axis=0, keepdims=True)
    return m, first


def _extract_top(cur_scr, n_chain, k):
    rows, n = cur_scr.shape[1], cur_scr.shape[2]
    idx = _iota((rows, n), 0)
    slot = _iota((k, n), 0)

    def body(i, carry):
        out = []
        for c in range(n_chain):
            vals, picks = carry[c]
            cur = cur_scr[c]
            m, first = _argmax_rows(cur)
            cur_scr[c] = jnp.where(idx == first, NEG_INF, cur)
            out.append((jnp.where(slot == i, m, vals), jnp.where(slot == i, first, picks)))
        return tuple(out)

    init = tuple((jnp.zeros((k, n), F32), jnp.zeros((k, n), jnp.int32)) for _ in range(n_chain))
    return lax.fori_loop(0, k, body, init)


def _candidate_pieces(k):
    assert k == 16
    pieces = [(0, 1, 16, 16), (1, 2, 8, 8)]
    pieces += [(a, a + 1, 8, k // (a + 1)) for a in range(2, 8)]
    pieces.append((8, 16, 1, 1))
    return pieces


def _peer_route_kernel(qt_ref, sk_ref, alpha_ref, count_ref, beta_ref, rank_ref, cur_scr, cand_scr):
    k = PEER_TOPK
    hd = sk_ref.shape[3]
    nk = sk_ref.shape[2]
    pieces = _candidate_pieces(k)
    n_lt = qt_ref.shape[1] // LANES
    idx = _iota((nk, LANES), 0)

    scores = []
    for lt in range(n_lt):
        lanes = slice(lt * LANES, (lt + 1) * LANES)
        for c in range(2):
            s = _dot(sk_ref[0, c], qt_ref[c * hd:(c + 1) * hd, lanes])
            cur_scr[2 * lt + c] = s
            scores.append(s)
    tops = _extract_top(cur_scr, 2 * n_lt, k)

    for lt in range(n_lt):
        v1, v2 = tops[2 * lt][0], tops[2 * lt + 1][0]
        cands = []
        for a_lo, a_hi, n_b, n_valid in pieces:
            if a_hi - a_lo == 1:
                c = v1[a_lo:a_hi, :] + v2[0:n_b, :]
                cands.append(jnp.where(_iota(c.shape, 0) < n_valid, c, NEG_INF))
            else:
                cands.append(v1[a_lo:a_hi, :] + v2[0:1, :])
        cand_scr[lt] = jnp.concatenate(cands, axis=0)
    picked = _extract_top(cand_scr, n_lt, k)

    for lt in range(n_lt):
        lanes = slice(lt * LANES, (lt + 1) * LANES)
        (v1, rows1), (v2, rows2) = tops[2 * lt], tops[2 * lt + 1]
        best, best_rows = picked[lt]
        z = jnp.sum(jnp.exp(best - best[0:1, :]), axis=0, keepdims=True)
        count = jnp.zeros((nk, LANES), F32)
        rank2 = jnp.full((nk, LANES), float(k), F32)
        row = 0
        for a_lo, a_hi, n_b, _ in pieces:
            for a in range(a_lo, a_hi):
                in_piece = jnp.logical_and(best_rows >= row, best_rows < row + n_b)
                n_a = jnp.sum(jnp.where(in_piece, 1.0, 0.0), axis=0, keepdims=True)
                count = jnp.where(idx == rows1[a:a + 1, :], n_a, count)
                row += n_b
        for b in range(k):
            rank2 = jnp.where(idx == rows2[b:b + 1, :], float(b), rank2)
        alpha_ref[0, :, lanes] = jnp.exp(scores[2 * lt] - v1[0:1, :]) / z
        count_ref[0, :, lanes] = count
        beta_ref[0, :, lanes] = jnp.exp(scores[2 * lt + 1] - v2[0:1, :])
        rank_ref[0, :, lanes] = rank2


def _peer_route(q_t, subkeys, *, tt):
    t = q_t.shape[1]
    nh, _, nk, hd = subkeys.shape
    assert t % tt == 0 and tt % LANES == 0 and nk % 8 == 0
    out = jax.ShapeDtypeStruct((nh, nk, t), F32)
    ospec = pl.BlockSpec((1, nk, tt), lambda i, h: (h, 0, i))
    n_cand = sum((a_hi - a_lo) * n_b for a_lo, a_hi, n_b, _ in _candidate_pieces(PEER_TOPK))
    return pl.pallas_call(
        _peer_route_kernel,
        name="peer_route",
        out_shape=[out, out, out, out],
        grid=(t // tt, nh),
        in_specs=[pl.BlockSpec((2 * hd, tt), lambda i, h: (h, i)),
                  pl.BlockSpec((1, 2, nk, hd), lambda i, h: (h, 0, 0, 0))],
        out_specs=[ospec, ospec, ospec, ospec],
        scratch_shapes=[pltpu.VMEM((2 * (tt // LANES), nk, LANES), F32),
                        pltpu.VMEM((tt // LANES, n_cand, LANES), F32)],
        compiler_params=_cparams("arbitrary", "arbitrary"),
    )(q_t, subkeys)


def _peer_dense_kernel(ht_ref, u_ref, vt_ref, alpha_ref, count_ref, beta_ref, rank_ref, o_ref, aw_scr):
    e = pl.program_id(1)
    nh, nk, tt = beta_ref.shape
    n_sub, _, sub = vt_ref.shape
    per_sub = sub // nk
    ib = n_sub * per_sub

    @pl.when(e == 0)
    def _():
        o_ref[...] = jnp.zeros(o_ref.shape, F32)

    i0 = pl.multiple_of(e * ib, ib)
    s_next = _dot(u_ref[0:sub, :], ht_ref[...])
    acc = None
    for c in range(n_sub):
        s_cur = s_next
        if c + 1 < n_sub:
            s_next = _dot(u_ref[(c + 1) * sub:(c + 2) * sub, :], ht_ref[...])
        for lt in range(tt // LANES):
            lanes = slice(lt * LANES, (lt + 1) * LANES)
            counts = [count_ref[h, pl.ds(i0, ib), lanes] for h in range(nh)]
            alphas = [alpha_ref[h, pl.ds(i0, ib), lanes] for h in range(nh)]
            for kk in range(per_sub):
                ii = c * per_sub + kk
                rows = slice(kk * nk, (kk + 1) * nk)
                w = jnp.zeros((nk, LANES), F32)
                for h in range(nh):
                    hit = rank_ref[h, :, lanes] < counts[h][ii:ii + 1, :]
                    w = w + jnp.where(hit, beta_ref[h, :, lanes], 0.0) * alphas[h][ii:ii + 1, :]
                aw_scr[c, rows, lanes] = (w * _gelu(s_cur[rows, lanes])).astype(BF16)
        part = _dot(vt_ref[c], aw_scr[c])
        acc = part if acc is None else acc + part
    o_ref[...] += acc


def _peer_dense(h_t, u_tabs, vt_tiles, layer, route, *, tt, tile_subs):
    d, t = h_t.shape
    alpha, count, beta, rank = route
    nh, nk, _ = alpha.shape
    sub = vt_tiles.shape[3]
    assert t % tt == 0 and (nk * nk) % (tile_subs * sub) == 0 and (tile_subs * sub // nk) % 8 == 0
    te = tile_subs * sub
    rspec = pl.BlockSpec((nh, nk, tt), lambda i, e: (0, 0, i))
    return pl.pallas_call(
        _peer_dense_kernel,
        name="peer_dense",
        out_shape=jax.ShapeDtypeStruct((d, t), F32),
        grid=(t // tt, nk * nk // te),
        in_specs=[pl.BlockSpec((d, tt), lambda i, e: (0, i)),
                  pl.BlockSpec((None, te, d), lambda i, e: (layer, e, 0)),
                  pl.BlockSpec((None, tile_subs, d, sub), lambda i, e: (layer, e, 0, 0)),
                  rspec, rspec, rspec, rspec],
        out_specs=pl.BlockSpec((d, tt), lambda i, e: (0, i)),
        scratch_shapes=[pltpu.VMEM((tile_subs, sub, tt), BF16)],
        compiler_params=_cparams("arbitrary", "arbitrary"),
    )(h_t, u_tabs, vt_tiles, alpha, count, beta, rank)


def _prep_weights(norm_mix_g, mod_mix_w, mod_mix_b, norm_ffn_g, mod_ffn_w, mod_ffn_b,
                  w_in_ab, w_out_ab, gmlp_ln_g, gmlp_ln_b, gmlp_ws, gmlp_bs,
                  ssd_w_in, ssd_conv_w, ssd_conv_b, ssd_dt_bias, ssd_a_log, ssd_d, ssd_norm_g, ssd_w_out,
                  peer_wq, peer_subkeys, peer_u, peer_v, final_norm_g):
    depth = norm_mix_g.shape[0]
    d = norm_mix_g.shape[1]
    p = dict(depth=depth, norm_mix_g=norm_mix_g, norm_ffn_g=norm_ffn_g, final_norm_g=final_norm_g,
             gmlp_ln_g=gmlp_ln_g, gmlp_ln_b=gmlp_ln_b, gmlp_ws=gmlp_ws, gmlp_bs=gmlp_bs,
             ssd_conv_w=ssd_conv_w, ssd_conv_b=ssd_conv_b, ssd_norm_g=ssd_norm_g)
    mod_w, mod_b = [], []
    for i in range(depth):
        mod_w += [mod_mix_w[i], mod_ffn_w[i]]
        mod_b += [mod_mix_b[i], mod_ffn_b[i]]
    p["mod_w"] = jnp.stack(mod_w).astype(BF16)
    p["mod_b"] = jnp.stack(mod_b)[:, None, :]
    p["w_in_ab"] = w_in_ab.astype(BF16)
    p["w_out_ab"] = w_out_ab.astype(BF16)
    n_ssd = ssd_w_in.shape[0]
    z_w = ssd_w_in[:, :, :SSD_INNER]
    xbc_w = ssd_w_in[:, :, SSD_INNER:SSD_INNER + SSD_CONV_DIM]
    dt_w = ssd_w_in[:, :, SSD_INNER + SSD_CONV_DIM:]
    used = 2 * SSD_INNER + 2 * SSD_BC + LANES
    total = -(-used // SSD_IN_COL_TILE) * SSD_IN_COL_TILE
    p["ssd_w_in"] = jnp.concatenate(
        [z_w, xbc_w, dt_w, jnp.zeros((n_ssd, d, total - used + LANES - SSD_HEADS), F32)], axis=-1).astype(BF16)
    pad_heads = lambda a, v: jnp.concatenate([a, jnp.full((n_ssd, LANES - SSD_HEADS), v, F32)], axis=-1)[:, None, :]
    p["ssd_dt_bias"] = pad_heads(ssd_dt_bias, 0.0)
    p["ssd_a_log"] = pad_heads(ssd_a_log, 0.0)
    p["ssd_dskip"] = jnp.repeat(ssd_d, SSD_HEAD_DIM, axis=-1)[:, None, :]
    p["ssd_w_out"] = ssd_w_out.astype(BF16)
    p["peer_wq"] = peer_wq.astype(BF16)
    p["peer_subkeys"] = peer_subkeys.astype(BF16)
    p["peer_u"] = peer_u.astype(BF16)
    n_keys = peer_subkeys.shape[3]
    sub = PEER_FIRST_KEYS_PER_SUBTILE * n_keys
    p["peer_vt"] = peer_v.astype(BF16).reshape(depth, n_keys * n_keys // sub, sub, d).transpose(0, 1, 3, 2)
    return p


def _peer_layer(x, mod, seq_len, p, i, *, tm, final):
    q_t, h_t = _adaln_matmul(x, p["norm_ffn_g"][i], mod, seq_len, p["peer_wq"][i], tm=tm,
                             tn=p["peer_wq"].shape[2] // 2, out_dtype=BF16, feature_major=True)
    nk = p["peer_subkeys"].shape[3]
    route = _peer_route(q_t, p["peer_subkeys"][i], tt=PEER_TOKEN_TILE)
    y_t = _peer_dense(h_t, p["peer_u"], p["peer_vt"], i, route, tt=PEER_TOKEN_TILE,
                      tile_subs=PEER_SUBTILES_PER_STEP)
    return _residual_t(x, y_t, mod, seq_len, tm=tm, final_g=p["final_norm_g"] if final else None)


def _ab_layer(x, mod, bsz, seq_len, pos, past, p, i, j, *, tm):
    t, d = x.shape
    w = GMLP_GROUPS * (d // 16)
    proj = _adaln_matmul(x, p["norm_mix_g"][i], mod, seq_len, p["w_in_ab"][j], tm=tm,
                         tn=p["w_in_ab"].shape[2] // 2)
    cos, sin = _rope_tables(pos)
    lc = min(seq_len, GMLP_CHUNK)
    ws = p["gmlp_ws"][j][:, :lc, :lc]
    bias = jnp.repeat(p["gmlp_bs"][j][:, :lc].T, w // GMLP_GROUPS, axis=1)
    if seq_len < GMLP_CHUNK:
        reps = tm // seq_len
        ws = jnp.where(jnp.tril(jnp.ones((lc, lc), bool)), ws, 0.0)
        eye = jnp.eye(reps, dtype=F32)
        ws = jnp.einsum("ab,gts->gatbs", eye, ws).reshape(GMLP_GROUPS, tm, tm)
        bias = jnp.tile(bias, (reps, 1))
        cos = jnp.tile(cos, (reps, 1))
        sin = jnp.tile(sin, (reps, 1))
        tr, lc_eff = tm, tm
    else:
        tr, lc_eff = MOBA_BLOCK, lc
    post = _ab_post(proj, cos, sin, p["gmlp_ln_g"][j], p["gmlp_ln_b"][j], ws.astype(BF16), bias,
                    tr=tr, lc=lc_eff, head_major=past is None, bsz=bsz)
    a_out, vln, k_rot, kmean = post[:4]
    v_new = proj[:, 2 * w + 2 * w:2 * w + 3 * w]
    nh, hd = MOBA_HEADS, MOBA_HEAD_DIM
    if past is None:
        nb = seq_len // MOBA_BLOCK
        qt_hm, k_blk, vt_blk = post[4:]
        km_hm = kmean.reshape(bsz, nb, nh, hd).transpose(0, 2, 1, 3).astype(BF16)
        o_t = _moba_prompt(qt_hm, k_blk, vt_blk, km_hm)
        b_out = o_t.transpose(0, 3, 1, 2).reshape(t, w)
    else:
        q_rot = post[4]
        page_table, cache_k, cache_v = past
        rows = -(-seq_len // 16) * 16

        def head_major(a):
            a = a.astype(BF16).reshape(bsz, seq_len, nh, hd).transpose(0, 2, 1, 3)
            return jnp.pad(a, ((0, 0), (0, 0), (0, rows - seq_len), (0, 0)))

        b_out = _moba_sample(page_table, head_major(q_rot), head_major(k_rot), head_major(v_new),
                             cache_k.transpose(0, 2, 3, 1), cache_v.transpose(0, 2, 3, 1), n_new=seq_len)
        b_out = b_out[:, :, :seq_len].transpose(0, 2, 1, 3).reshape(t, w).astype(BF16)
    x = _proj_residual([a_out, b_out], p["w_out_ab"][j], x, mod, seq_len, tm=tm)
    lcv = min(seq_len, GMLP_CHUNK)
    gv = vln.reshape(bsz, seq_len, w)[:, seq_len - lcv:]
    return (x, k_rot.reshape(bsz, seq_len, nh, hd), v_new.reshape(bsz, seq_len, nh, hd), gv)


def _ssd_layer(x, mod, bsz, seq_len, ssm0, conv0, p, i, j, *, tm):
    t, d = x.shape
    proj = _adaln_matmul(x, p["norm_mix_g"][i], mod, seq_len, p["ssd_w_in"][j], tm=tm, tn=SSD_IN_COL_TILE)
    if conv0 is None:
        conv0 = jnp.zeros((bsz, SSD_CONV - 1, SSD_CONV_DIM), F32)
    if ssm0 is None:
        st0_t = jnp.zeros((bsz, SSD_STATE, SSD_INNER), F32)
    else:
        st0_t = ssm0.reshape(bsz, SSD_INNER, SSD_STATE).transpose(0, 2, 1)
    conv0p = jnp.pad(conv0, ((0, 0), (8 - (SSD_CONV - 1), 0), (0, 0)))
    y, st_t, conv_new = _ssd_scan(proj, conv0p, st0_t, p["ssd_conv_w"][j], p["ssd_conv_b"][j],
                                  p["ssd_dt_bias"][j], p["ssd_a_log"][j], p["ssd_dskip"][j],
                                  p["ssd_norm_g"][j], bsz=bsz, seq_len=seq_len)
    x = _proj_residual([y], p["ssd_w_out"][j], x, mod, seq_len, tm=tm)
    ssm_new = st_t.transpose(0, 2, 1).reshape(bsz, SSD_HEADS, SSD_HEAD_DIM, SSD_STATE)
    return x, ssm_new, conv_new


def _run_trunk(x, c, pos0, past_kv, ssm_past, conv_past, p):
    bsz, seq_len, d = x.shape
    t = bsz * seq_len
    tm = _token_tile(t, seq_len)
    assert t % tm == 0
    x = x.reshape(t, d)
    pos = pos0 + jnp.arange(seq_len, dtype=jnp.int32)
    mods = _mod_all(c, p["mod_w"], p["mod_b"])
    depth = p["depth"]
    k_rows, v_rows, gv_rows, ssm_new, conv_new = [], [], [], [], []
    for i in range(depth):
        j = i // 2
        if i % 2 == 0:
            past = None if past_kv is None else (past_kv[0], past_kv[1][j], past_kv[2][j])
            x, k_new, v_new, gv = _ab_layer(x, mods[2 * i], bsz, seq_len, pos, past, p, i, j, tm=tm)
            k_rows.append(k_new)
            v_rows.append(v_new)
            gv_rows.append(gv)
        else:
            x, s_new, cv_new = _ssd_layer(x, mods[2 * i], bsz, seq_len,
                                          None if ssm_past is None else ssm_past[j],
                                          None if conv_past is None else conv_past[j], p, i, j, tm=tm)
            ssm_new.append(s_new)
            conv_new.append(cv_new)
        x = _peer_layer(x, mods[2 * i + 1], seq_len, p, i, tm=tm, final=(i == depth - 1))
    y = x.reshape(bsz, seq_len, d)
    return y, jnp.stack(k_rows), jnp.stack(v_rows), jnp.stack(gv_rows), jnp.stack(ssm_new), jnp.stack(conv_new)


def kernel(x_prompt, x_sample, cache_k, cache_v, page_table, state_ssm, state_conv, c_prompt, c_sample,
           norm_mix_g, mod_mix_w, mod_mix_b, norm_ffn_g, mod_ffn_w, mod_ffn_b,
           w_in_ab, w_out_ab, gmlp_ln_g, gmlp_ln_b, gmlp_ws, gmlp_bs,
           ssd_w_in, ssd_conv_w, ssd_conv_b, ssd_dt_bias, ssd_a_log, ssd_d, ssd_norm_g, ssd_w_out,
           peer_wq, peer_subkeys, peer_u, peer_v, final_norm_g):
    p = _prep_weights(norm_mix_g, mod_mix_w, mod_mix_b, norm_ffn_g, mod_ffn_w, mod_ffn_b,
                      w_in_ab, w_out_ab, gmlp_ln_g, gmlp_ln_b, gmlp_ws, gmlp_bs,
                      ssd_w_in, ssd_conv_w, ssd_conv_b, ssd_dt_bias, ssd_a_log, ssd_d, ssd_norm_g, ssd_w_out,
                      peer_wq, peer_subkeys, peer_u, peer_v, final_norm_g)
    prompt = _run_trunk(x_prompt, c_prompt, 0, None, None, None, p)
    past_len = page_table.shape[1] * PAGE_SIZE
    sample = _run_trunk(x_sample, c_sample, past_len, (page_table, cache_k, cache_v),
                        state_ssm, state_conv, p)
    return (prompt[0], sample[0]) + prompt[1:] + sample[1:]
```
